```python
import math
import jax, jax.numpy as jnp
from jax import lax
import numpy as np

D_MODEL = 4096
BATCH = 4
SEQ = 4096
DEPTH = 1

HY_WIDTH = D_MODEL // 2
HY_ORDER = 2
HY_SHORT = 3
HY_POS_EMB = 33
HY_FILTER_HIDDEN = 64
HY_DECAY_TARGET = 1e-2
HY_FAST_DECAY = 0.3
HY_SLOW_DECAY = 1.5
HY_MOD_SHIFT = 0.05
HY_FILTER_OUT_SCALE = 0.1
ML_HEADS = 8
ML_DV = (D_MODEL // 2) // ML_HEADS
ML_DK = ML_DV // 2
ML_WIDTH = ML_HEADS * ML_DV
ML_CHUNK = 64
COL_Q = (HY_ORDER + 1) * HY_WIDTH
COL_K = COL_Q + ML_HEADS * ML_DK
COL_V = COL_K + ML_HEADS * ML_DK
COL_O = COL_V + ML_WIDTH
COL_IF = COL_O + ML_WIDTH
COL_GATE = COL_IF + 4 * ML_HEADS
N_COLS = COL_GATE + 2 * D_MODEL
MOE_GROUPS = 8
MOE_PER_GROUP = 8
MOE_EXPERTS = MOE_GROUPS * MOE_PER_GROUP
MOE_TOPK = 2
MOE_HIDDEN = (D_MODEL * 3) // 16
MOE_BLOCK = 128
DEEPNORM_ALPHA = (2.0 * DEPTH) ** 0.25
DEEPNORM_BETA = (8.0 * DEPTH) ** -0.25
LN_EPS = 1e-5

kernel_name = 'hybrid_hyena_mlstm_hmoe_deepnorm'


def layer_norm(x, g, b):
    xf = x.astype(jnp.float32)
    mu = jnp.mean(xf, axis=-1, keepdims=True)
    var = jnp.mean(jnp.square(xf - mu), axis=-1, keepdims=True)
    return ((xf - mu) * lax.rsqrt(var + LN_EPS) * g + b).astype(x.dtype)


def short_conv_centred(u, w, b):
    pad = HY_SHORT // 2
    L = u.shape[1]
    up = jnp.pad(u, ((0, 0), (pad, pad), (0, 0)))
    out = b
    for j in range(HY_SHORT):
        out = out + up[:, j:j + L] * w[j]
    return out


def hyena_filters(L, f_w1, f_b1, f_fr1, f_w2, f_b2, f_fr2, f_w3, f_b3, f_fr3, f_wout):
    f32 = jnp.float32
    t = jnp.linspace(0.0, 1.0, L, dtype=f32)[:, None]
    bands = (HY_POS_EMB - 1) // 2
    freqs = jnp.linspace(1e-4, bands - 1, bands, dtype=f32)[None, :]
    w = (2.0 * math.pi / L) * jnp.arange(L, dtype=f32)[:, None]
    feats = jnp.concatenate([t, jnp.cos(freqs * w), -jnp.sin(freqs * w)], axis=-1)
    h = jnp.sin(f_fr1 * (feats @ f_w1 + f_b1))
    h = jnp.sin(f_fr2 * (h @ f_w2 + f_b2))
    h = jnp.sin(f_fr3 * (h @ f_w3 + f_b3))
    h = (h @ f_wout).astype(f32).reshape(L, 2, HY_ORDER, HY_WIDTH)
    deltas = jnp.abs(jnp.linspace(math.log(HY_DECAY_TARGET) / HY_SLOW_DECAY,
                                  math.log(HY_DECAY_TARGET) / HY_FAST_DECAY,
                                  HY_WIDTH, dtype=f32))
    window = jnp.exp(-t * deltas[None, :]) + HY_MOD_SHIFT
    return h * window[:, None, None, :]


def bidir_long_conv(u, h_fwd, h_bwd):
    L, C = h_fwd.shape
    k = jnp.concatenate([h_fwd, jnp.zeros((1, C), h_fwd.dtype), h_bwd[:0:-1]], axis=0)
    U = jnp.fft.rfft(u.astype(jnp.float32), n=2 * L, axis=1)
    K = jnp.fft.rfft(k, n=2 * L, axis=0)
    y = jnp.fft.irfft(U * K[None], n=2 * L, axis=1)[:, :L]
    return y.astype(u.dtype)


def hyena_branch(z_hy, conv_w, conv_b, filt, bias):
    u = short_conv_centred(z_hy, conv_w, conv_b)
    x1, x2, v = jnp.split(u, 3, axis=-1)
    for o, gate in enumerate((x1, x2)):
        v = gate * (bidir_long_conv(v, filt[:, 0, o], filt[:, 1, o]) + v * bias[o])
    return v


def mlstm_direction(q, k, v, log_i, log_f):
    B, H, L, _ = q.shape
    T = ML_CHUNK
    nC = L // T

    def chunks(a):
        return jnp.moveaxis(a.reshape(a.shape[:2] + (nC, T) + a.shape[3:]), 2, 0)

    xs = (chunks(q), chunks(k), chunks(v), chunks(log_i), chunks(log_f))
    within = jnp.tril(jnp.ones((T, T), dtype=bool))

    def step(carry, inp):
        C, n, m = carry
        qc, kc, vc, li, lf = inp
        b = jnp.cumsum(lf, axis=-1)
        d_log = jnp.where(within, b[..., :, None] - b[..., None, :] + li[..., None, :], -jnp.inf)
        inter = b + m[..., None]
        m_t = jnp.maximum(inter, jnp.max(d_log, axis=-1))
        scores = jnp.einsum('bhtd,bhsd->bhts', qc, kc) * jnp.exp(d_log - m_t[..., None])
        w_inter = jnp.exp(inter - m_t)
        num = (jnp.einsum('bhts,bhsv->bhtv', scores, vc)
               + w_inter[..., None] * jnp.einsum('bhtd,bhdv->bhtv', qc, C))
        den = jnp.sum(scores, axis=-1) + w_inter * jnp.einsum('bhtd,bhd->bht', qc, n)
        h = num / jnp.maximum(jnp.abs(den), jnp.exp(-m_t))[..., None]
        b_last = b[..., -1]
        w_log = b_last[..., None] - b + li
        m_new = jnp.maximum(b_last + m, jnp.max(w_log, axis=-1))
        kw = kc * jnp.exp(w_log - m_new[..., None])[..., None]
        decay = jnp.exp(b_last + m - m_new)
        C_new = decay[..., None, None] * C + jnp.einsum('bhsd,bhsv->bhdv', kw, vc)
        n_new = decay[..., None] * n + jnp.sum(kw, axis=2)
        return (C_new, n_new, m_new), h

    f32 = jnp.float32
    init = (jnp.zeros((B, H, ML_DK, ML_DV), f32), jnp.zeros((B, H, ML_DK), f32), jnp.zeros((B, H), f32))
    _, hs = lax.scan(step, init, xs)
    return jnp.moveaxis(hs, 0, 2).reshape(B, H, L, ML_DV)


def token_mixers(x, w_in, hy_conv_w, hy_conv_b, hy_f_w1, hy_f_b1, hy_f_fr1, hy_f_w2, hy_f_b2,
                 hy_f_fr2, hy_f_w3, hy_f_b3, hy_f_fr3, hy_f_wout, hy_bias, ml_gate_bias,
                 ml_norm_g, p_hy, p_ml, w_out):
    B, L, _ = x.shape
    f32 = jnp.float32
    z = x @ w_in
    filt = hyena_filters(L, hy_f_w1, hy_f_b1, hy_f_fr1, hy_f_w2, hy_f_b2, hy_f_fr2,
                         hy_f_w3, hy_f_b3, hy_f_fr3, hy_f_wout)
    y_hy = hyena_branch(z[..., :COL_Q], hy_conv_w, hy_conv_b, filt, hy_bias).astype(x.dtype)
    def heads(a, d):
        return a.reshape(B, L, ML_HEADS, d).transpose(0, 2, 1, 3).astype(f32)
    q = heads(z[..., COL_Q:COL_K], ML_DK) * (ML_DK ** -0.5)
    k = heads(z[..., COL_K:COL_V], ML_DK)
    v = heads(z[..., COL_V:COL_O], ML_DV)
    o = z[..., COL_O:COL_IF].astype(f32)
    g = (z[..., COL_IF:COL_GATE].astype(f32).reshape(B, L, 4, ML_HEADS) + ml_gate_bias).transpose(2, 0, 3, 1)
    rev = lambda a: jnp.flip(a, axis=2)
    h_fwd = mlstm_direction(q, k, v, g[0], jax.nn.log_sigmoid(g[1]))
    h_bwd = rev(mlstm_direction(rev(q), rev(k), rev(v), rev(g[2]), rev(jax.nn.log_sigmoid(g[3]))))
    h = h_fwd + h_bwd
    mu = jnp.mean(h, axis=-1, keepdims=True)
    var = jnp.mean(jnp.square(h - mu), axis=-1, keepdims=True)
    hn = ((h - mu) * lax.rsqrt(var + LN_EPS)).transpose(0, 2, 1, 3).reshape(B, L, ML_WIDTH) * ml_norm_g
    y_ml = (jax.nn.sigmoid(o) * hn).astype(x.dtype)
    gate_hy = jax.nn.sigmoid(z[..., COL_GATE:COL_GATE + D_MODEL])
    gate_ml = jax.nn.sigmoid(z[..., COL_GATE + D_MODEL:])
    merged = gate_hy * (y_hy @ p_hy) + gate_ml * (y_ml @ p_ml)
    return merged @ w_out


def hierarchical_moe(x, router_w1, router_b1, router_w2, router_b2, exp_w1, exp_w3, exp_w2):
    B, L, D = x.shape
    N = B * L
    f32 = jnp.float32
    xt = x.reshape(N, D)
    lg1 = (xt @ router_w1).astype(f32) + router_b1
    p1 = jax.nn.softmax(lg1, axis=-1)
    g_sel = jnp.argmax(lg1, axis=-1)
    p_group = jnp.take_along_axis(p1, g_sel[:, None], axis=1)[:, 0]
    lg2 = ((xt @ router_w2).astype(f32) + router_b2).reshape(N, MOE_GROUPS, MOE_PER_GROUP)
    lg2 = jnp.take_along_axis(lg2, g_sel[:, None, None], axis=1)[:, 0]
    top_p, top_j = lax.top_k(jax.nn.softmax(lg2, axis=-1), MOE_TOPK)
    top_p = top_p / jnp.sum(top_p, axis=-1, keepdims=True)
    weight = p_group[:, None] * top_p
    eid = g_sel[:, None].astype(jnp.int32) * MOE_PER_GROUP + top_j.astype(jnp.int32)
    M = N * MOE_TOPK
    eid_f = eid.reshape(M)
    tok_f = jnp.repeat(jnp.arange(N, dtype=jnp.int32), MOE_TOPK)
    w_f = weight.reshape(M)
    order = jnp.argsort(eid_f)
    e_sorted = eid_f[order]
    counts = jnp.bincount(eid_f, length=MOE_EXPERTS)
    padded = ((counts + MOE_BLOCK - 1) // MOE_BLOCK) * MOE_BLOCK
    start = jnp.cumsum(counts) - counts
    pend = jnp.cumsum(padded)
    pstart = pend - padded
    dest = pstart[e_sorted] + (jnp.arange(M, dtype=jnp.int32) - start[e_sorted])
    n_blocks = -(-M // MOE_BLOCK) + MOE_EXPERTS
    n_slots = n_blocks * MOE_BLOCK
    slot_tok = jnp.full((n_slots,), N, jnp.int32).at[dest].set(tok_f[order])
    slot_w = jnp.zeros((n_slots,), f32).at[dest].set(w_f[order])
    blk_e = jnp.clip(jnp.searchsorted(pend, jnp.arange(n_blocks) * MOE_BLOCK, side='right'), 0, MOE_EXPERTS - 1)
    x_pad = jnp.concatenate([xt, jnp.zeros((1, D), xt.dtype)], axis=0)

    def expert_block(args):
        toks, e = args
        xb = x_pad[toks]
        hb = jax.nn.silu(xb @ exp_w1[e]) * (xb @ exp_w3[e])
        return hb @ exp_w2[e]

    yb = lax.map(expert_block, (slot_tok.reshape(n_blocks, MOE_BLOCK), blk_e))
    contrib = yb.reshape(n_slots, D) * slot_w[:, None].astype(yb.dtype)
    y = jnp.zeros((N + 1, D), x.dtype).at[slot_tok].add(contrib.astype(x.dtype))[:N]
    return y.reshape(B, L, D)


def setup_inputs(seed: int = 0) -> dict:
    key = jax.random.key(seed)
    ks = jax.random.split(key, 32)
    f32 = jnp.float32
    nrm = lambda k, shape, s: jax.random.normal(k, shape, f32) * s
    D = D_MODEL
    fh = HY_FILTER_HIDDEN
    gate_bias = jnp.stack([
        nrm(ks[16], (ML_HEADS,), 0.1),
        jnp.linspace(3.0, 6.0, ML_HEADS, dtype=f32) + nrm(ks[17], (ML_HEADS,), 0.1),
        nrm(ks[18], (ML_HEADS,), 0.1),
        jnp.linspace(3.0, 6.0, ML_HEADS, dtype=f32) + nrm(ks[19], (ML_HEADS,), 0.1),
    ])
    return {
        'x': nrm(ks[0], (BATCH, SEQ, D), 1.0),
        'w_in': nrm(ks[1], (D, N_COLS), D ** -0.5),
        'hy_conv_w': nrm(ks[2], (HY_SHORT, (HY_ORDER + 1) * HY_WIDTH), HY_SHORT ** -0.5),
        'hy_conv_b': nrm(ks[3], ((HY_ORDER + 1) * HY_WIDTH,), 0.01),
        'hy_f_w1': nrm(ks[4], (HY_POS_EMB, fh), HY_POS_EMB ** -0.5),
        'hy_f_b1': nrm(ks[5], (fh,), 0.01),
        'hy_f_fr1': 1.0 + nrm(ks[6], (fh,), 0.01),
        'hy_f_w2': nrm(ks[7], (fh, fh), fh ** -0.5),
        'hy_f_b2': nrm(ks[8], (fh,), 0.01),
        'hy_f_fr2': 1.0 + nrm(ks[9], (fh,), 0.01),
        'hy_f_w3': nrm(ks[10], (fh, fh), fh ** -0.5),
        'hy_f_b3': nrm(ks[11], (fh,), 0.01),
        'hy_f_fr3': 1.0 + nrm(ks[12], (fh,), 0.01),
        'hy_f_wout': nrm(ks[13], (fh, 2 * HY_ORDER * HY_WIDTH), HY_FILTER_OUT_SCALE * fh ** -0.5),
        'hy_bias': nrm(ks[14], (HY_ORDER, HY_WIDTH), 1.0),
        'ml_gate_bias': gate_bias,
        'ml_norm_g': 1.0 + nrm(ks[15], (ML_WIDTH,), 0.02),
        'p_hy': nrm(ks[20], (HY_WIDTH, D), DEEPNORM_BETA * HY_WIDTH ** -0.5),
        'p_ml': nrm(ks[21], (ML_WIDTH, D), DEEPNORM_BETA * ML_WIDTH ** -0.5),
        'w_out': nrm(ks[22], (D, D), DEEPNORM_BETA * D ** -0.5),
        'ln1_g': 1.0 + nrm(ks[23], (D,), 0.02),
        'ln1_b': nrm(ks[24], (D,), 0.01),
        'router_w1': nrm(ks[25], (D, MOE_GROUPS), D ** -0.5),
        'router_b1': nrm(ks[26], (MOE_GROUPS,), 0.01),
        'router_w2': nrm(ks[27], (D, MOE_EXPERTS), D ** -0.5),
        'router_b2': nrm(ks[28], (MOE_EXPERTS,), 0.01),
        'exp_w1': nrm(ks[29], (MOE_EXPERTS, D, MOE_HIDDEN), D ** -0.5),
        'exp_w3': nrm(ks[30], (MOE_EXPERTS, D, MOE_HIDDEN), D ** -0.5),
        'exp_w2': nrm(ks[31], (MOE_EXPERTS, MOE_HIDDEN, D), DEEPNORM_BETA * MOE_HIDDEN ** -0.5),
        'ln2_g': 1.0 + nrm(jax.random.fold_in(key, 101), (D,), 0.02),
        'ln2_b': nrm(jax.random.fold_in(key, 102), (D,), 0.01),
    }


def reference(x, w_in, hy_conv_w, hy_conv_b, hy_f_w1, hy_f_b1, hy_f_fr1, hy_f_w2, hy_f_b2,
              hy_f_fr2, hy_f_w3, hy_f_b3, hy_f_fr3, hy_f_wout, hy_bias, ml_gate_bias, ml_norm_g,
              p_hy, p_ml, w_out, ln1_g, ln1_b, router_w1, router_b1, router_w2, router_b2,
              exp_w1, exp_w3, exp_w2, ln2_g, ln2_b):
    h = x
    for _ in range(DEPTH):
        mix = token_mixers(h, w_in, hy_conv_w, hy_conv_b, hy_f_w1, hy_f_b1, hy_f_fr1, hy_f_w2,
                           hy_f_b2, hy_f_fr2, hy_f_w3, hy_f_b3, hy_f_fr3, hy_f_wout, hy_bias,
                           ml_gate_bias, ml_norm_g, p_hy, p_ml, w_out)
        h = layer_norm(DEEPNORM_ALPHA * h + mix, ln1_g, ln1_b)
        ffn = hierarchical_moe(h, router_w1, router_b1, router_w2, router_b2, exp_w1, exp_w3, exp_w2)
        h = layer_norm(DEEPNORM_ALPHA * h + ffn, ln2_g, ln2_b)
    return h
```

```python
import functools
import math

import jax
import jax.numpy as jnp
from jax import lax
from jax.experimental import pallas as pl
from jax.experimental.pallas import tpu as pltpu

F32 = jnp.float32
BF16 = jnp.bfloat16
HIGHEST = lax.Precision.HIGHEST

VMEM_LIMIT_BYTES = 56 * 1024 * 1024
LANES = 128

HY_ORDER = 2
HY_SHORT = 3
HY_POS_EMB = 33
HY_DECAY_TARGET = 1e-2
HY_FAST_DECAY = 0.3
HY_SLOW_DECAY = 1.5
HY_MOD_SHIFT = 0.05
ML_HEADS = 8
MOE_GROUPS = 8
MOE_PER_GROUP = 8
MOE_TOPK = 2
DEPTH = 1
DEEPNORM_ALPHA = (2.0 * DEPTH) ** 0.25
LN_EPS = 1e-5

FFT_N1 = 64
HY_CB = 128
ML_CHUNK = 256
MOE_BM = 128
MOE_TB = 256


def _cparams(*sem):
    return pltpu.CompilerParams(dimension_semantics=sem, vmem_limit_bytes=VMEM_LIMIT_BYTES)


def _mm_body(a_ref, b_ref, o_ref):
    o_ref[...] = jnp.dot(a_ref[...], b_ref[...], preferred_element_type=F32).astype(o_ref.dtype)


def _matmul(a, b, out_dtype, tm, tn):
    m, k = a.shape
    _, n = b.shape
    assert m % tm == 0 and n % tn == 0
    return pl.pallas_call(
        _mm_body,
        grid=(m // tm, n // tn),
        in_specs=[pl.BlockSpec((tm, k), lambda i, j: (i, 0)),
                  pl.BlockSpec((k, tn), lambda i, j: (0, j))],
        out_specs=pl.BlockSpec((tm, tn), lambda i, j: (i, j)),
        out_shape=jax.ShapeDtypeStruct((m, n), out_dtype),
        compiler_params=_cparams("parallel", "arbitrary"),
        name="proj_matmul",
    )(a, b)


def _short_conv_body(z_ref, w_ref, b_ref, o_ref):
    z = z_ref[0]
    L = z.shape[0]
    row = lax.broadcasted_iota(jnp.int32, z.shape, 0)
    prev = jnp.where(row == 0, 0.0, pltpu.roll(z, 1, 0))
    nxt = jnp.where(row == L - 1, 0.0, pltpu.roll(z, L - 1, 0))
    o_ref[0] = b_ref[...] + prev * w_ref[0:1, :] + z * w_ref[1:2, :] + nxt * w_ref[2:3, :]


def _short_conv(z, w, b, cb=512):
    B, L, C = z.shape
    return pl.pallas_call(
        _short_conv_body,
        grid=(B, C // cb),
        in_specs=[pl.BlockSpec((1, L, cb), lambda i, j: (i, 0, j)),
                  pl.BlockSpec((HY_SHORT, cb), lambda i, j: (0, j)),
                  pl.BlockSpec((1, cb), lambda i, j: (0, j))],
        out_specs=pl.BlockSpec((1, L, cb), lambda i, j: (i, 0, j)),
        out_shape=jax.ShapeDtypeStruct((B, L, C), F32),
        compiler_params=_cparams("parallel", "parallel"),
        name="hy_short_conv",
    )(z, w, b.reshape(1, C))


def _filter_hidden_body(w1_ref, b1_ref, fr1_ref, w2_ref, b2_ref, fr2_ref, w3_ref, b3_ref, fr3_ref,
                        o_ref, *, L, rows):
    i = pl.program_id(0)
    n = i * rows + lax.broadcasted_iota(jnp.int32, (rows, 1), 0)
    pos = jnp.where(n < L, n, 2 * L - n).astype(F32)
    t = pos / (L - 1.0)
    w = (2.0 * math.pi / L) * pos
    lane = lax.broadcasted_iota(jnp.int32, (1, LANES), 1)
    bands = (HY_POS_EMB - 1) // 2
    band = jnp.where(lane <= bands, lane - 1, lane - 1 - bands).astype(F32)
    freq = 1e-4 + band * ((bands - 1 - 1e-4) / (bands - 1))
    ang = w * freq
    feats = jnp.where(lane == 0, t,
                      jnp.where(lane <= bands, jnp.cos(ang),
                                jnp.where(lane <= 2 * bands, -jnp.sin(ang), 0.0)))
    h = jnp.sin(fr1_ref[...] * (jnp.dot(feats, w1_ref[...], precision=HIGHEST,
                                        preferred_element_type=F32) + b1_ref[...]))
    h = jnp.sin(fr2_ref[...] * (jnp.dot(h, w2_ref[...], precision=HIGHEST,
                                        preferred_element_type=F32) + b2_ref[...]))
    h = jnp.sin(fr3_ref[...] * (jnp.dot(h, w3_ref[...], precision=HIGHEST,
                                        preferred_element_type=F32) + b3_ref[...]))
    o_ref[...] = h


def _filter_out_body(h_ref, wout_ref, delta_ref, o_ref, *, L):
    d = pl.program_id(1)
    h = jnp.dot(h_ref[...], wout_ref[0], precision=HIGHEST, preferred_element_type=F32)
    r = lax.broadcasted_iota(jnp.int32, (L, 1), 0)
    pos = jnp.where(d == 0, r, L - r).astype(F32)
    t = pos / (L - 1.0)
    window = jnp.exp(-t * delta_ref[...]) + HY_MOD_SHIFT
    keep = jnp.logical_or(d == 0, r > 0)
    o_ref[0] = jnp.where(keep, h * window, 0.0)


def _hyena_filters(L, C, f_w1, f_b1, f_fr1, f_w2, f_b2, f_fr2, f_w3, f_b3, f_fr3, f_wout, cb=512):
    fh = f_w2.shape[0]
    rows = 1024 if (2 * L) % 1024 == 0 else 2 * L
    w1p = jnp.zeros((LANES, fh), F32).at[:HY_POS_EMB].set(f_w1)
    vec = lambda a: a.reshape(1, fh)
    full = lambda shape: pl.BlockSpec(shape, lambda i: (0,) * len(shape))
    hid = pl.pallas_call(
        functools.partial(_filter_hidden_body, L=L, rows=rows),
        grid=(2 * L // rows,),
        in_specs=[full((LANES, fh)), full((1, fh)), full((1, fh)),
                  full((fh, fh)), full((1, fh)), full((1, fh)),
                  full((fh, fh)), full((1, fh)), full((1, fh))],
        out_specs=pl.BlockSpec((rows, fh), lambda i: (i, 0)),
        out_shape=jax.ShapeDtypeStruct((2 * L, fh), F32),
        compiler_params=_cparams("parallel"),
        name="hy_filter_hidden",
    )(w1p, vec(f_b1), vec(f_fr1), f_w2, vec(f_b2), vec(f_fr2), f_w3, vec(f_b3), vec(f_fr3))
    wout = f_wout.reshape(fh, 2, HY_ORDER, C).transpose(1, 2, 0, 3).reshape(2 * HY_ORDER, fh, C)
    deltas = jnp.abs(jnp.linspace(math.log(HY_DECAY_TARGET) / HY_SLOW_DECAY,
                                  math.log(HY_DECAY_TARGET) / HY_FAST_DECAY, C, dtype=F32))
    cb = min(cb, C)
    return pl.pallas_call(
        functools.partial(_filter_out_body, L=L),
        grid=(HY_ORDER, 2, C // cb),
        in_specs=[pl.BlockSpec((L, fh), lambda o, d, j: (d, 0)),
                  pl.BlockSpec((1, fh, cb), lambda o, d, j: (d * HY_ORDER + o, 0, j)),
                  pl.BlockSpec((1, cb), lambda o, d, j: (0, j))],
        out_specs=pl.BlockSpec((1, L, cb), lambda o, d, j: (o, d, j)),
        out_shape=jax.ShapeDtypeStruct((HY_ORDER, 2 * L, C), F32),
        compiler_params=_cparams("parallel", "parallel", "parallel"),
        name="hy_filter_out",
    )(hid, wout, deltas.reshape(1, C))


def _dft_tables(L):
    N = 2 * L
    N1 = FFT_N1
    N2 = N // N1
    h = N1 // 2
    n2 = jnp.arange(N2, dtype=jnp.int32)[:, None, None]
    k1 = jnp.arange(N1, dtype=jnp.int32)[None, :, None]
    n1 = jnp.arange(N1, dtype=jnp.int32)[None, None, :]
    ph = (k1 * (N2 * n1 + n2)) % N
    ang = ph.astype(F32) * (-2.0 * math.pi / N)
    mr, mi = jnp.cos(ang), jnp.sin(ang)
    mrp, mip = mr[:, :, :h], mi[:, :, :h]
    t1 = jnp.concatenate([jnp.concatenate([mrp, -mip], axis=2),
                          jnp.concatenate([mip, mrp], axis=2)], axis=1)
    mrt, mit = jnp.swapaxes(mrp, 1, 2) / N, jnp.swapaxes(mip, 1, 2) / N
    t1i = jnp.concatenate([jnp.concatenate([mrt, mit], axis=2),
                           jnp.concatenate([-mit, mrt], axis=2)], axis=1)
    t1f = jnp.concatenate([mr, mi], axis=1)
    a = jnp.arange(N2, dtype=jnp.int32)
    ang2 = ((a[:, None] * a[None, :]) % N2).astype(F32) * (-2.0 * math.pi / N2)
    fr, fi = jnp.cos(ang2), jnp.sin(ang2)
    t2 = jnp.concatenate([jnp.concatenate([fr, -fi], axis=1),
                          jnp.concatenate([fi, fr], axis=1)], axis=0)
    t2i = jnp.concatenate([jnp.concatenate([fr, fi], axis=1),
                           jnp.concatenate([-fi, fr], axis=1)], axis=0)
    return t1, t1i, t1f, t2, t2i


def _dotf(a, b):
    return jnp.dot(a, b, precision=HIGHEST, preferred_element_type=F32)


def _filter_fft_body(k_ref, t1f_ref, t2_ref, o_ref, sr, si, *, N1, N2):
    def stage1(n2, c):
        a = _dotf(t1f_ref[n2], k_ref[0, pl.ds(n2, N1, stride=N2), :])
        sr[pl.ds(n2, N1, stride=N2), :] = a[:N1]
        si[pl.ds(n2, N1, stride=N2), :] = a[N1:]
        return c
    lax.fori_loop(0, N2, stage1, 0)

    def stage2(k1, c):
        r0 = pl.multiple_of(k1 * N2, N2)
        x = _dotf(t2_ref[...], jnp.concatenate([sr[pl.ds(r0, N2), :], si[pl.ds(r0, N2), :]], axis=0))
        o_ref[0, 0, pl.ds(r0, N2), :] = x[:N2]
        o_ref[0, 1, pl.ds(r0, N2), :] = x[N2:]
        return c
    lax.fori_loop(0, N1, stage2, 0)


def _filter_fft(kern, t1f, t2):
    O, N, C = kern.shape
    N1, N2 = FFT_N1, N // FFT_N1
    cb = min(HY_CB, C)
    full = lambda shape: pl.BlockSpec(shape, lambda o, j: (0,) * len(shape),
                                      pipeline_mode=pl.Buffered(1))
    return pl.pallas_call(
        functools.partial(_filter_fft_body, N1=N1, N2=N2),
        grid=(O, C // cb),
        in_specs=[pl.BlockSpec((1, N, cb), lambda o, j: (o, 0, j)),
                  full(t1f.shape), full(t2.shape)],
        out_specs=pl.BlockSpec((1, 2, N, cb), lambda o, j: (o, 0, 0, j)),
        out_shape=jax.ShapeDtypeStruct((O, 2, N, C), F32),
        scratch_shapes=[pltpu.VMEM((N, cb), F32), pltpu.VMEM((N, cb), F32)],
        compiler_params=_cparams("parallel", "parallel"),
        name="hy_filter_fft",
    )(kern, t1f, t2)


def _long_conv_body(v_ref, g_ref, kf_ref, bias_ref, t1_ref, t1i_ref, t2_ref, t2i_ref, o_ref, sr, si,
                    *, N1, N2):
    h = N1 // 2

    def stage1(n2, c):
        rhs = jnp.concatenate([v_ref[0, pl.ds(n2, h, stride=N2), :],
                               v_ref[1, pl.ds(n2, h, stride=N2), :]], axis=0)
        a = _dotf(t1_ref[n2], rhs)
        sr[pl.ds(n2, N1, stride=N2), :] = a[:N1]
        si[pl.ds(n2, N1, stride=N2), :] = a[N1:]
        return c
    lax.fori_loop(0, N2, stage1, 0)

    def stage2(k1, c):
        r0 = pl.multiple_of(k1 * N2, N2)
        x = _dotf(t2_ref[...], jnp.concatenate([sr[pl.ds(r0, N2), :], si[pl.ds(r0, N2), :]], axis=0))
        xr, xi = x[:N2], x[N2:]
        kr = kf_ref[0, 0, pl.ds(r0, N2), :]
        ki = kf_ref[0, 1, pl.ds(r0, N2), :]
        p = jnp.concatenate([xr * kr - xi * ki, xr * ki + xi * kr], axis=0)
        y = _dotf(t2i_ref[...], p)
        sr[pl.ds(r0, N2), :] = y[:N2]
        si[pl.ds(r0, N2), :] = y[N2:]
        return c
    lax.fori_loop(0, N1, stage2, 0)

    bias = bias_ref[...]

    def stage3(n2, c):
        rhs = jnp.concatenate([sr[pl.ds(n2, N1, stride=N2), :],
                               si[pl.ds(n2, N1, stride=N2), :]], axis=0)
        y = _dotf(t1i_ref[n2], rhs)
        for b in range(2):
            v = v_ref[b, pl.ds(n2, h, stride=N2), :]
            g = g_ref[b, pl.ds(n2, h, stride=N2), :]
            o_ref[b, pl.ds(n2, h, stride=N2), :] = g * (y[b * h:(b + 1) * h] + v * bias)
        return c
    lax.fori_loop(0, N2, stage3, 0)


def _long_conv(v_arr, v_off, g_arr, g_off, kfreq, order, bias, tables, C):
    B, L, _ = v_arr.shape
    t1, t1i, _, t2, t2i = tables
    N = 2 * L
    N1, N2 = FFT_N1, N // FFT_N1
    cb = min(HY_CB, C)
    nj = C // cb
    full = lambda shape: pl.BlockSpec(shape, lambda j, p: (0,) * len(shape),
                                      pipeline_mode=pl.Buffered(1))
    return pl.pallas_call(
        functools.partial(_long_conv_body, N1=N1, N2=N2),
        grid=(nj, B // 2),
        in_specs=[pl.BlockSpec((2, L, cb), lambda j, p: (p, 0, v_off * nj + j)),
                  pl.BlockSpec((2, L, cb), lambda j, p: (p, 0, g_off * nj + j)),
                  pl.BlockSpec((1, 2, N, cb), lambda j, p: (order, 0, 0, j),
                               pipeline_mode=pl.Buffered(1)),
                  pl.BlockSpec((None, 1, cb), lambda j, p: (order, 0, j)),
                  full(t1.shape), full(t1i.shape), full(t2.shape), full(t2i.shape)],
        out_specs=pl.BlockSpec((2, L, cb), lambda j, p: (p, 0, j)),
        out_shape=jax.ShapeDtypeStruct((B, L, C), F32),
        scratch_shapes=[pltpu.VMEM((N, cb), F32), pltpu.VMEM((N, cb), F32)],
        compiler_params=_cparams("parallel", "arbitrary"),
        name="hy_long_conv",
    )(v_arr, g_arr, kfreq, bias.reshape(HY_ORDER, 1, C), t1, t1i, t2, t2i)


def _log_sigmoid(x):
    return jnp.minimum(x, 0.0) - jnp.log1p(jnp.exp(-jnp.abs(x)))


def _mlstm_body(*refs, rev, T, scale, final):
    if final:
        (q_ref, k_ref, v_ref, gc_ref, gr_ref, bc_ref, br_ref, hprev_ref, o_ref, ng_ref,
         out_ref, c_s, n_s, m_s) = refs
    else:
        q_ref, k_ref, v_ref, gc_ref, gr_ref, bc_ref, br_ref, out_ref, c_s, n_s, m_s = refs

    @pl.when(pl.program_id(2) == 0)
    def _():
        c_s[...] = jnp.zeros_like(c_s)
        n_s[...] = jnp.zeros_like(n_s)
        m_s[...] = jnp.zeros_like(m_s)

    gi = 2 if rev else 0
    gc = gc_ref[...] + bc_ref[...]
    gr = gr_ref[...] + br_ref[...]
    li_c, lf_c = gc[:, gi:gi + 1], _log_sigmoid(gc[:, gi + 1:gi + 2])
    li_r, lf_r = gr[gi:gi + 1, :], _log_sigmoid(gr[gi + 1:gi + 2, :])
    row = lax.broadcasted_iota(jnp.int32, (T, T), 0)
    col = lax.broadcasted_iota(jnp.int32, (T, T), 1)
    valid = (col >= row) if rev else (col <= row)
    valid_t = (row >= col) if rev else (row <= col)
    b_c = jnp.sum(jnp.where(valid, lf_r, 0.0), axis=1, keepdims=True)
    b_r = jnp.sum(jnp.where(valid_t, lf_c, 0.0), axis=0, keepdims=True)
    m = m_s[...]
    d = jnp.where(valid, b_c - b_r + li_r, -jnp.inf)
    inter = b_c + m
    m_t = jnp.maximum(inter, jnp.max(d, axis=1, keepdims=True))
    q, k, v = q_ref[...], k_ref[...], v_ref[...]
    qk = lax.dot_general(q, k, (((1,), (1,)), ((), ())), preferred_element_type=F32)
    s = qk * scale * jnp.exp(d - m_t)
    w_inter = jnp.exp(inter - m_t)
    qc = jnp.dot(q, c_s[...].astype(BF16), preferred_element_type=F32) * scale
    num = jnp.dot(s.astype(BF16), v, preferred_element_type=F32) + w_inter * qc
    qn = jnp.sum(q.astype(F32) * n_s[...], axis=1, keepdims=True) * scale
    den = jnp.sum(s, axis=1, keepdims=True) + w_inter * qn
    hout = num / jnp.maximum(jnp.abs(den), jnp.exp(-m_t))

    b_last = b_c[0:1, :] if rev else b_c[T - 1:T, :]
    w_c = b_last - b_c + li_c
    m_new = jnp.maximum(b_last + m, jnp.max(w_c, axis=0, keepdims=True))
    kw = k.astype(F32) * jnp.exp(w_c - m_new)
    decay = jnp.exp(b_last + m - m_new)
    c_s[...] = decay * c_s[...] + lax.dot_general(
        kw.astype(BF16), v, (((0,), (0,)), ((), ())), preferred_element_type=F32)
    n_s[...] = decay * n_s[...] + jnp.sum(kw, axis=0, keepdims=True)
    m_s[...] = m_new

    if final:
        hsum = hout + hprev_ref[...]
        mu = jnp.mean(hsum, axis=1, keepdims=True)
        var = jnp.mean(jnp.square(hsum - mu), axis=1, keepdims=True)
        hn = (hsum - mu) * lax.rsqrt(var + LN_EPS) * ng_ref[...]
        out_ref[...] = (jax.nn.sigmoid(o_ref[...].astype(F32)) * hn).astype(out_ref.dtype)
    else:
        out_ref[...] = hout


def _mlstm(zq, gates_c, gates_r, bias_c, bias_r, B, L, dk, dv, rev, hprev=None, norm_g=None):
    H = ML_HEADS
    T = min(ML_CHUNK, L)
    nc = L // T
    final = hprev is not None
    cidx = (lambda c: nc - 1 - c) if rev else (lambda c: c)
    kq, kv = H * dk // dk, (2 * H * dk) // dv
    in_specs = [
        pl.BlockSpec((T, dk), lambda b, h, c: (b * nc + cidx(c), h)),
        pl.BlockSpec((T, dk), lambda b, h, c: (b * nc + cidx(c), kq + h)),
        pl.BlockSpec((T, dv), lambda b, h, c: (b * nc + cidx(c), kv + h)),
        pl.BlockSpec((None, None, T, 4), lambda b, h, c: (b, h, cidx(c), 0)),
        pl.BlockSpec((None, None, 4, T), lambda b, h, c: (b, h, 0, cidx(c))),
        pl.BlockSpec((None, 1, 4), lambda b, h, c: (h, 0, 0)),
        pl.BlockSpec((None, 4, 1), lambda b, h, c: (h, 0, 0)),
    ]
    args = [zq, zq, zq, gates_c, gates_r, bias_c, bias_r]
    if final:
        in_specs += [
            pl.BlockSpec((T, dv), lambda b, h, c: (b * nc + cidx(c), h)),
            pl.BlockSpec((T, dv), lambda b, h, c: (b * nc + cidx(c), kv + H + h)),
            pl.BlockSpec((1, dv), lambda b, h, c: (0, h)),
        ]
        args += [hprev, zq, norm_g]
    return pl.pallas_call(
        functools.partial(_mlstm_body, rev=rev, T=T, scale=dk ** -0.5, final=final),
        grid=(B, H, nc),
        in_specs=in_specs,
        out_specs=pl.BlockSpec((T, dv), lambda b, h, c: (b * nc + cidx(c), h)),
        out_shape=jax.ShapeDtypeStruct((B * L, H * dv), BF16 if final else F32),
        scratch_shapes=[pltpu.VMEM((dk, dv), F32), pltpu.VMEM((1, dk), F32), pltpu.VMEM((1, 1), F32)],
        compiler_params=_cparams("parallel", "parallel", "arbitrary"),
        name="mlstm_bwd" if rev else "mlstm_fwd",
    )(*args)


def _merge_body(yh_ref, ym_ref, ph_ref, pm_ref, gh_ref, gm_ref, o_ref):
    a = jnp.dot(yh_ref[...], ph_ref[...], preferred_element_type=F32)
    b = jnp.dot(ym_ref[...], pm_ref[...], preferred_element_type=F32)
    o_ref[...] = (jax.nn.sigmoid(gh_ref[...].astype(F32)) * a
                  + jax.nn.sigmoid(gm_ref[...].astype(F32)) * b).astype(o_ref.dtype)


def _merge(y_hy, y_ml, p_hy, p_ml, gates, tm=512, tn=512):
    n, kh = y_hy.shape
    km = y_ml.shape[1]
    d = p_hy.shape[1]
    nj = d // tn
    return pl.pallas_call(
        _merge_body,
        grid=(n // tm, nj),
        in_specs=[pl.BlockSpec((tm, kh), lambda i, j: (i, 0)),
                  pl.BlockSpec((tm, km), lambda i, j: (i, 0)),
                  pl.BlockSpec((kh, tn), lambda i, j: (0, j)),
                  pl.BlockSpec((km, tn), lambda i, j: (0, j)),
                  pl.BlockSpec((tm, tn), lambda i, j: (i, j)),
                  pl.BlockSpec((tm, tn), lambda i, j: (i, nj + j))],
        out_specs=pl.BlockSpec((tm, tn), lambda i, j: (i, j)),
        out_shape=jax.ShapeDtypeStruct((n, d), BF16),
        compiler_params=_cparams("parallel", "arbitrary"),
        name="gated_merge",
    )(y_hy, y_ml, p_hy, p_ml, gates, gates)


def _layer_norm(x, g, b):
    mu = jnp.mean(x, axis=-1, keepdims=True)
    var = jnp.mean(jnp.square(x - mu), axis=-1, keepdims=True)
    return (x - mu) * lax.rsqrt(var + LN_EPS) * g + b


def _proj_ln_body(a_ref, w_ref, x_ref, g_ref, b_ref, o_ref, acc):
    k = pl.program_id(1)

    @pl.when(k == 0)
    def _():
        acc[...] = jnp.zeros_like(acc)

    acc[...] += jnp.dot(a_ref[...], w_ref[...], preferred_element_type=F32)

    @pl.when(k == pl.num_programs(1) - 1)
    def _():
        o_ref[...] = _layer_norm(DEEPNORM_ALPHA * x_ref[...] + acc[...], g_ref[...], b_ref[...])


def _proj_ln(a, w, x, g, b, tm=256, tk=512):
    n, kd = a.shape
    d = w.shape[1]
    return pl.pallas_call(
        _proj_ln_body,
        grid=(n // tm, kd // tk),
        in_specs=[pl.BlockSpec((tm, tk), lambda i, k: (i, k)),
                  pl.BlockSpec((tk, d), lambda i, k: (k, 0)),
                  pl.BlockSpec((tm, d), lambda i, k: (i, 0)),
                  pl.BlockSpec((1, d), lambda i, k: (0, 0)),
                  pl.BlockSpec((1, d), lambda i, k: (0, 0))],
        out_specs=pl.BlockSpec((tm, d), lambda i, k: (i, 0)),
        out_shape=jax.ShapeDtypeStruct((n, d), F32),
        scratch_shapes=[pltpu.VMEM((tm, d), F32)],
        compiler_params=_cparams("parallel", "arbitrary"),
        name="out_proj_ln",
    )(a, w, x, g.reshape(1, d), b.reshape(1, d))


def _router_body(x_ref, w_ref, b_ref, wout_ref, eout_ref):
    G, PG = MOE_GROUPS, MOE_PER_GROUP
    logits = jnp.dot(x_ref[...], w_ref[...], precision=HIGHEST, preferred_element_type=F32) + b_ref[...]
    lane = lax.broadcasted_iota(jnp.int32, logits.shape, 1)
    ninf = -jnp.inf
    first = lambda mask: jnp.min(jnp.where(mask, lane, 2 * LANES), axis=1, keepdims=True)
    lg1 = jnp.where(lane < G, logits, ninf)
    m1 = jnp.max(lg1, axis=1, keepdims=True)
    g_sel = first(lg1 == m1)
    p_group = 1.0 / jnp.sum(jnp.exp(lg1 - m1), axis=1, keepdims=True)
    lo = G + g_sel * PG
    in_grp = jnp.logical_and(lane >= lo, lane < lo + PG)
    lg2 = jnp.where(in_grp, logits, ninf)
    m2 = jnp.max(lg2, axis=1, keepdims=True)
    e2 = jnp.exp(lg2 - m2)
    p2 = jnp.where(in_grp, e2 / jnp.sum(e2, axis=1, keepdims=True), -1.0)
    t1 = jnp.max(p2, axis=1, keepdims=True)
    j1 = first(p2 == t1)
    p2b = jnp.where(lane == j1, -1.0, p2)
    t2 = jnp.max(p2b, axis=1, keepdims=True)
    j2 = first(p2b == t2)
    tot = t1 + t2
    wout_ref[...] = jnp.where(lane == 0, p_group * (t1 / tot),
                              jnp.where(lane == 1, p_group * (t2 / tot), 0.0))
    eout_ref[...] = jnp.where(lane == 0, j1 - G, jnp.where(lane == 1, j2 - G, 0))


def _router(x, router_w1, router_b1, router_w2, router_b2, tm=256):
    n, d = x.shape
    ncol = MOE_GROUPS + MOE_GROUPS * MOE_PER_GROUP
    w = jnp.zeros((d, LANES), F32).at[:, :ncol].set(jnp.concatenate([router_w1, router_w2], axis=1))
    b = jnp.zeros((1, LANES), F32).at[0, :ncol].set(jnp.concatenate([router_b1, router_b2]))
    return pl.pallas_call(
        _router_body,
        grid=(n // tm,),
        in_specs=[pl.BlockSpec((tm, d), lambda i: (i, 0)),
                  pl.BlockSpec((d, LANES), lambda i: (0, 0)),
                  pl.BlockSpec((1, LANES), lambda i: (0, 0))],
        out_specs=[pl.BlockSpec((tm, LANES), lambda i: (i, 0)),
                   pl.BlockSpec((tm, LANES), lambda i: (i, 0))],
        out_shape=[jax.ShapeDtypeStruct((n, LANES), F32), jax.ShapeDtypeStruct((n, LANES), jnp.int32)],
        compiler_params=_cparams("parallel"),
        name="moe_router",
    )(x, w, b)


def _row_copy(src_hbm, row, dst_vmem, r, sem):
    return pltpu.make_async_copy(src_hbm.at[pl.ds(row, 1), :], dst_vmem.at[pl.ds(r, 1), :], sem)


def _expert_body(tok_ref, blk_e_ref, nused_ref, x_hbm, w1_ref, w3_ref, w2_ref, o_ref, xbuf, sem, *, bm):
    del blk_e_ref
    i = pl.program_id(0)

    @pl.when(i < nused_ref[0])
    def _():
        def issue(r, c):
            _row_copy(x_hbm, tok_ref[i * bm + r], xbuf, r, sem).start()
            return c
        lax.fori_loop(0, bm, issue, 0)

        def wait(r, c):
            _row_copy(x_hbm, 0, xbuf, r, sem).wait()
            return c
        lax.fori_loop(0, bm, wait, 0)
        xb = xbuf[...].astype(BF16)
        a = jnp.dot(xb, w1_ref[...], preferred_element_type=F32)
        b = jnp.dot(xb, w3_ref[...], preferred_element_type=F32)
        hb = (a * jax.nn.sigmoid(a)) * b
        o_ref[...] = jnp.dot(hb.astype(BF16), w2_ref[...], preferred_element_type=F32)

    @pl.when(i >= nused_ref[0])
    def _():
        o_ref[...] = jnp.zeros_like(o_ref)


def _experts(x, slot_tok, blk_e, n_used, w1, w3, w2, bm):
    n, d = x.shape
    e, _, hd = w1.shape
    n_blocks = blk_e.shape[0]
    grid_spec = pltpu.PrefetchScalarGridSpec(
        num_scalar_prefetch=3,
        grid=(n_blocks,),
        in_specs=[pl.BlockSpec(memory_space=pl.ANY),
                  pl.BlockSpec((None, d, hd), lambda i, tok, be, nu: (be[i], 0, 0)),
                  pl.BlockSpec((None, d, hd), lambda i, tok, be, nu: (be[i], 0, 0)),
                  pl.BlockSpec((None, hd, d), lambda i, tok, be, nu: (be[i], 0, 0))],
        out_specs=pl.BlockSpec((bm, d), lambda i, tok, be, nu: (i, 0)),
        scratch_shapes=[pltpu.VMEM((bm, d), F32), pltpu.SemaphoreType.DMA(())],
    )
    return pl.pallas_call(
        functools.partial(_expert_body, bm=bm),
        grid_spec=grid_spec,
        out_shape=jax.ShapeDtypeStruct((n_blocks * bm, d), F32),
        compiler_params=_cparams("arbitrary"),
        name="moe_experts",
    )(slot_tok, blk_e, n_used, x, w1, w3, w2)


def _combine_body(slot_ref, y_hbm, w_ref, x_ref, g_ref, b_ref, o_ref, buf, sem, *, tb):
    i = pl.program_id(0)

    def issue(r, c):
        for kk in range(MOE_TOPK):
            _row_copy(y_hbm, slot_ref[(i * tb + r) * MOE_TOPK + kk], buf.at[kk], r, sem).start()
        return c
    lax.fori_loop(0, tb, issue, 0)

    def wait(r, c):
        for kk in range(MOE_TOPK):
            _row_copy(y_hbm, 0, buf.at[kk], r, sem).wait()
        return c
    lax.fori_loop(0, tb, wait, 0)
    w = w_ref[...]
    y = w[:, 0:1] * buf[0] + w[:, 1:2] * buf[1]
    o_ref[...] = _layer_norm(DEEPNORM_ALPHA * x_ref[...] + y, g_ref[...], b_ref[...])


def _combine_ln(slot_of, yb, weights, x, g, b, tb):
    n, d = x.shape
    grid_spec = pltpu.PrefetchScalarGridSpec(
        num_scalar_prefetch=1,
        grid=(n // tb,),
        in_specs=[pl.BlockSpec(memory_space=pl.ANY),
                  pl.BlockSpec((tb, LANES), lambda i, s: (i, 0)),
                  pl.BlockSpec((tb, d), lambda i, s: (i, 0)),
                  pl.BlockSpec((1, d), lambda i, s: (0, 0)),
                  pl.BlockSpec((1, d), lambda i, s: (0, 0))],
        out_specs=pl.BlockSpec((tb, d), lambda i, s: (i, 0)),
        scratch_shapes=[pltpu.VMEM((MOE_TOPK, tb, d), F32), pltpu.SemaphoreType.DMA(())],
    )
    return pl.pallas_call(
        functools.partial(_combine_body, tb=tb),
        grid_spec=grid_spec,
        out_shape=jax.ShapeDtypeStruct((n, d), F32),
        compiler_params=_cparams("arbitrary"),
        name="moe_combine_ln",
    )(slot_of, yb, weights, x, g.reshape(1, d), b.reshape(1, d))


def _moe(h1, router_w1, router_b1, router_w2, router_b2, exp_w1, exp_w3, exp_w2, ln_g, ln_b):
    n, d = h1.shape
    e = exp_w1.shape[0]
    bm = MOE_BM
    weights, eids = _router(h1, router_w1, router_b1, router_w2, router_b2)
    m = n * MOE_TOPK
    eid_f = eids[:, :MOE_TOPK].reshape(m)
    onehot = (eid_f[:, None] == jnp.arange(e, dtype=jnp.int32)[None, :]).astype(jnp.int32)
    csum = jnp.cumsum(onehot, axis=0)
    rank = jnp.sum(csum * onehot, axis=1) - 1
    counts = csum[-1]
    nblk_e = (counts + bm - 1) // bm
    bend = jnp.cumsum(nblk_e)
    pstart = (bend - nblk_e) * bm
    slot_of = (pstart[eid_f] + rank).astype(jnp.int32)
    n_blocks = -(-m // bm) + e
    tok_f = jnp.arange(m, dtype=jnp.int32) // MOE_TOPK
    slot_tok = jnp.zeros((n_blocks * bm,), jnp.int32).at[slot_of].set(tok_f)
    n_used = bend[-1:].astype(jnp.int32)
    blk = jnp.minimum(jnp.arange(n_blocks, dtype=jnp.int32), n_used[0] - 1)
    blk_e = jnp.minimum(jnp.searchsorted(bend, blk, side='right'), e - 1).astype(jnp.int32)
    yb = _experts(h1, slot_tok, blk_e, n_used, exp_w1.astype(BF16), exp_w3.astype(BF16),
                  exp_w2.astype(BF16), bm)
    return _combine_ln(slot_of, yb, weights, h1, ln_g, ln_b, min(MOE_TB, n))


def kernel(x, w_in, hy_conv_w, hy_conv_b, hy_f_w1, hy_f_b1, hy_f_fr1, hy_f_w2, hy_f_b2, hy_f_fr2,
           hy_f_w3, hy_f_b3, hy_f_fr3, hy_f_wout, hy_bias, ml_gate_bias, ml_norm_g, p_hy, p_ml, w_out,
           ln1_g, ln1_b, router_w1, router_b1, router_w2, router_b2, exp_w1, exp_w3, exp_w2,
           ln2_g, ln2_b):
    B, L, D = x.shape
    N = B * L
    C = D // 2
    H = ML_HEADS
    dv = C // H
    dk = dv // 2
    col_q = (HY_ORDER + 1) * C
    col_if = col_q + 2 * H * dk + 2 * C
    col_gate = col_if + 4 * H

    xf = x.reshape(N, D)
    xb = xf.astype(BF16)
    w_hy = w_in[:, :col_q].astype(BF16)
    w_ml = w_in[:, col_q:col_if].astype(BF16)
    w_if = jnp.zeros((D, LANES), BF16).at[:, :4 * H].set(w_in[:, col_if:col_gate].astype(BF16))
    w_gt = w_in[:, col_gate:].astype(BF16)

    z_hy = _matmul(xb, w_hy, F32, 1024, 512).reshape(B, L, col_q)
    z_ml = _matmul(xb, w_ml, BF16, 1024, 512)
    z_if = _matmul(xb, w_if, F32, 1024, LANES)
    z_gt = _matmul(xb, w_gt, BF16, 1024, 512)

    tables = _dft_tables(L)
    kern = _hyena_filters(L, C, hy_f_w1, hy_f_b1, hy_f_fr1, hy_f_w2, hy_f_b2, hy_f_fr2,
                          hy_f_w3, hy_f_b3, hy_f_fr3, hy_f_wout)
    kfreq = _filter_fft(kern, tables[2], tables[3])
    u = _short_conv(z_hy, hy_conv_w, hy_conv_b)
    v1 = _long_conv(u, 2, u, 0, kfreq, 0, hy_bias, tables, C)
    y_hy = _long_conv(v1, 0, u, 1, kfreq, 1, hy_bias, tables, C).reshape(N, C).astype(BF16)

    g = z_if[:, :4 * H].reshape(B, L, 4, H)
    gates_c = g.transpose(0, 3, 1, 2)
    gates_r = g.transpose(0, 3, 2, 1)
    bias_c = ml_gate_bias.T.reshape(H, 1, 4)
    bias_r = ml_gate_bias.T.reshape(H, 4, 1)
    h_fwd = _mlstm(z_ml, gates_c, gates_r, bias_c, bias_r, B, L, dk, dv, rev=False)
    y_ml = _mlstm(z_ml, gates_c, gates_r, bias_c, bias_r, B, L, dk, dv, rev=True,
                  hprev=h_fwd, norm_g=ml_norm_g.reshape(1, C))

    merged = _merge(y_hy, y_ml, p_hy.astype(BF16), p_ml.astype(BF16), z_gt)
    h1 = _proj_ln(merged, w_out.astype(BF16), xf, ln1_g, ln1_b)
    out = _moe(h1, router_w1, router_b1, router_w2, router_b2, exp_w1, exp_w3, exp_w2, ln2_g, ln2_b)
    return out.reshape(B, L, D)
```

```python
import functools
import math

import jax
import jax.numpy as jnp
from jax import lax
from jax.experimental import pallas as pl
from jax.experimental.pallas import tpu as pltpu

F32 = jnp.float32
BF16 = jnp.bfloat16
HIGHEST = lax.Precision.HIGHEST

VMEM_LIMIT_BYTES = 56 * 1024 * 1024
LANES = 128

HY_ORDER = 2
HY_SHORT = 3
HY_POS_EMB = 33
HY_DECAY_TARGET = 1e-2
HY_FAST_DECAY = 0.3
HY_SLOW_DECAY = 1.5
HY_MOD_SHIFT = 0.05
ML_HEADS = 8
MOE_GROUPS = 8
MOE_PER_GROUP = 8
MOE_TOPK = 2
DEPTH = 1
DEEPNORM_ALPHA = (2.0 * DEPTH) ** 0.25
LN_EPS = 1e-5

FFT_N1 = 64
HY_CB = 256
FFT_G = 16
FFT_KC = 8
ML_CHUNK = 256
MOE_BM = 128
MOE_TB = 256


def _cparams(*sem):
    return pltpu.CompilerParams(dimension_semantics=sem, vmem_limit_bytes=VMEM_LIMIT_BYTES)


def _mm_body(a_ref, b_ref, o_ref):
    o_ref[...] = jnp.dot(a_ref[...], b_ref[...], preferred_element_type=F32).astype(o_ref.dtype)


def _matmul(a, b, out_dtype, tm, tn):
    m, k = a.shape
    _, n = b.shape
    assert m % tm == 0 and n % tn == 0
    return pl.pallas_call(
        _mm_body,
        grid=(m // tm, n // tn),
        in_specs=[pl.BlockSpec((tm, k), lambda i, j: (i, 0)),
                  pl.BlockSpec((k, tn), lambda i, j: (0, j))],
        out_specs=pl.BlockSpec((tm, tn), lambda i, j: (i, j)),
        out_shape=jax.ShapeDtypeStruct((m, n), out_dtype),
        compiler_params=_cparams("parallel", "arbitrary"),
        name="proj_matmul",
    )(a, b)


def _permute_rows(src_ref, dst_ref, inner, outer):
    def it(a, c):
        rows = src_ref[0, pl.ds(a, inner, stride=outer), :]
        dst_ref[0, pl.ds(pl.multiple_of(a * inner, inner), inner), :] = rows.astype(dst_ref.dtype)
        return c
    lax.fori_loop(0, outer, it, 0)


def _short_conv_body(z_ref, w_ref, b_ref, o_ref, tmp, *, n2):
    z = z_ref[0]
    L = z.shape[0]
    row = lax.broadcasted_iota(jnp.int32, z.shape, 0)
    prev = jnp.where(row == 0, 0.0, pltpu.roll(z, 1, 0))
    nxt = jnp.where(row == L - 1, 0.0, pltpu.roll(z, L - 1, 0))
    tmp[0] = b_ref[...] + prev * w_ref[0:1, :] + z * w_ref[1:2, :] + nxt * w_ref[2:3, :]
    _permute_rows(tmp, o_ref, L // n2, n2)


def _short_conv(z, w, b, n2, cb=LANES):
    B, L, C = z.shape
    return pl.pallas_call(
        functools.partial(_short_conv_body, n2=n2),
        grid=(B, C // cb),
        in_specs=[pl.BlockSpec((1, L, cb), lambda i, j: (i, 0, j)),
                  pl.BlockSpec((HY_SHORT, cb), lambda i, j: (0, j)),
                  pl.BlockSpec((1, cb), lambda i, j: (0, j))],
        out_specs=pl.BlockSpec((1, L, cb), lambda i, j: (i, 0, j)),
        out_shape=jax.ShapeDtypeStruct((B, L, C), F32),
        scratch_shapes=[pltpu.VMEM((1, L, cb), F32)],
        compiler_params=_cparams("parallel", "parallel"),
        name="hy_short_conv",
    )(z, w, b.reshape(1, C))


def _row_permute_body(x_ref, o_ref, *, inner, outer):
    _permute_rows(x_ref, o_ref, inner, outer)


def _row_permute(x, inner, outer, out_dtype, cb=LANES):
    A, R, C = x.shape
    assert R == inner * outer
    cb = min(cb, C)
    return pl.pallas_call(
        functools.partial(_row_permute_body, inner=inner, outer=outer),
        grid=(A, C // cb),
        in_specs=[pl.BlockSpec((1, R, cb), lambda i, j: (i, 0, j))],
        out_specs=pl.BlockSpec((1, R, cb), lambda i, j: (i, 0, j)),
        out_shape=jax.ShapeDtypeStruct((A, R, C), out_dtype),
        compiler_params=_cparams("parallel", "parallel"),
        name="hy_row_permute",
    )(x)


def _filter_hidden_body(w1_ref, b1_ref, fr1_ref, w2_ref, b2_ref, fr2_ref, w3_ref, b3_ref, fr3_ref,
                        o_ref, *, L, rows):
    i = pl.program_id(0)
    n = i * rows + lax.broadcasted_iota(jnp.int32, (rows, 1), 0)
    pos = jnp.where(n < L, n, 2 * L - n).astype(F32)
    t = pos / (L - 1.0)
    w = (2.0 * math.pi / L) * pos
    lane = lax.broadcasted_iota(jnp.int32, (1, LANES), 1)
    bands = (HY_POS_EMB - 1) // 2
    band = jnp.where(lane <= bands, lane - 1, lane - 1 - bands).astype(F32)
    freq = 1e-4 + band * ((bands - 1 - 1e-4) / (bands - 1))
    ang = w * freq
    feats = jnp.where(lane == 0, t,
                      jnp.where(lane <= bands, jnp.cos(ang),
                                jnp.where(lane <= 2 * bands, -jnp.sin(ang), 0.0)))
    h = jnp.sin(fr1_ref[...] * (jnp.dot(feats, w1_ref[...], precision=HIGHEST,
                                        preferred_element_type=F32) + b1_ref[...]))
    h = jnp.sin(fr2_ref[...] * (jnp.dot(h, w2_ref[...], precision=HIGHEST,
                                        preferred_element_type=F32) + b2_ref[...]))
    h = jnp.sin(fr3_ref[...] * (jnp.dot(h, w3_ref[...], precision=HIGHEST,
                                        preferred_element_type=F32) + b3_ref[...]))
    o_ref[...] = h


def _filter_out_body(h_ref, wout_ref, delta_ref, bias_ref, o_ref, *, L):
    d = pl.program_id(1)
    h = jnp.dot(h_ref[...], wout_ref[0], precision=HIGHEST, preferred_element_type=F32)
    r = lax.broadcasted_iota(jnp.int32, (L, 1), 0)
    pos = jnp.where(d == 0, r, L - r).astype(F32)
    t = pos / (L - 1.0)
    window = jnp.exp(-t * delta_ref[...]) + HY_MOD_SHIFT
    first = r == 0
    tap = jnp.where(jnp.logical_and(d == 1, first), 0.0, h * window)
    o_ref[0] = tap + jnp.where(jnp.logical_and(d == 0, first), bias_ref[...], 0.0)


def _hyena_filters(L, C, f_w1, f_b1, f_fr1, f_w2, f_b2, f_fr2, f_w3, f_b3, f_fr3, f_wout, bias, cb=512):
    fh = f_w2.shape[0]
    rows = 1024 if (2 * L) % 1024 == 0 else 2 * L
    w1p = jnp.zeros((LANES, fh), F32).at[:HY_POS_EMB].set(f_w1)
    vec = lambda a: a.reshape(1, fh)
    full = lambda shape: pl.BlockSpec(shape, lambda i: (0,) * len(shape))
    hid = pl.pallas_call(
        functools.partial(_filter_hidden_body, L=L, rows=rows),
        grid=(2 * L // rows,),
        in_specs=[full((LANES, fh)), full((1, fh)), full((1, fh)),
                  full((fh, fh)), full((1, fh)), full((1, fh)),
                  full((fh, fh)), full((1, fh)), full((1, fh))],
        out_specs=pl.BlockSpec((rows, fh), lambda i: (i, 0)),
        out_shape=jax.ShapeDtypeStruct((2 * L, fh), F32),
        compiler_params=_cparams("parallel"),
        name="hy_filter_hidden",
    )(w1p, vec(f_b1), vec(f_fr1), f_w2, vec(f_b2), vec(f_fr2), f_w3, vec(f_b3), vec(f_fr3))
    wout = f_wout.reshape(fh, 2, HY_ORDER, C).transpose(1, 2, 0, 3).reshape(2 * HY_ORDER, fh, C)
    deltas = jnp.abs(jnp.linspace(math.log(HY_DECAY_TARGET) / HY_SLOW_DECAY,
                                  math.log(HY_DECAY_TARGET) / HY_FAST_DECAY, C, dtype=F32))
    cb = min(cb, C)
    return pl.pallas_call(
        functools.partial(_filter_out_body, L=L),
        grid=(HY_ORDER, 2, C // cb),
        in_specs=[pl.BlockSpec((L, fh), lambda o, d, j: (d, 0)),
                  pl.BlockSpec((1, fh, cb), lambda o, d, j: (d * HY_ORDER + o, 0, j)),
                  pl.BlockSpec((1, cb), lambda o, d, j: (0, j)),
                  pl.BlockSpec((None, 1, cb), lambda o, d, j: (o, 0, j))],
        out_specs=pl.BlockSpec((1, L, cb), lambda o, d, j: (o, d, j)),
        out_shape=jax.ShapeDtypeStruct((HY_ORDER, 2 * L, C), F32),
        compiler_params=_cparams("parallel", "parallel", "parallel"),
        name="hy_filter_out",
    )(hid, wout, deltas.reshape(1, C), bias.reshape(HY_ORDER, 1, C))


def _dft_tables(L):
    N = 2 * L
    N1 = FFT_N1
    N2 = N // N1
    h = N1 // 2
    n2 = jnp.arange(N2, dtype=jnp.int32)[:, None, None]
    k1 = jnp.arange(N1, dtype=jnp.int32)[None, :, None]
    n1 = jnp.arange(N1, dtype=jnp.int32)[None, None, :]
    ph = (k1 * (N2 * n1 + n2)) % N
    ang = ph.astype(F32) * (-2.0 * math.pi / N)
    mr, mi = jnp.cos(ang), jnp.sin(ang)
    mrp, mip = mr[:, :, :h], mi[:, :, :h]
    t1 = jnp.concatenate([jnp.concatenate([mrp, -mip], axis=2),
                          jnp.concatenate([mip, mrp], axis=2)], axis=1)
    mrt, mit = jnp.swapaxes(mrp, 1, 2) / N, jnp.swapaxes(mip, 1, 2) / N
    t1i = jnp.concatenate([jnp.concatenate([mrt, mit], axis=2),
                           jnp.concatenate([-mit, mrt], axis=2)], axis=1)
    t1f = jnp.concatenate([mr, mi], axis=1)
    a = jnp.arange(N2, dtype=jnp.int32)
    ang2 = ((a[:, None] * a[None, :]) % N2).astype(F32) * (-2.0 * math.pi / N2)
    fr, fi = jnp.cos(ang2), jnp.sin(ang2)
    t2 = jnp.concatenate([jnp.concatenate([fr, -fi], axis=1),
                          jnp.concatenate([fi, fr], axis=1)], axis=0)
    t2i = jnp.concatenate([jnp.concatenate([fr, fi], axis=1),
                           jnp.concatenate([-fi, fr], axis=1)], axis=0)
    return tuple(_hi_lo_rows(t) for t in (t1, t1i, t1f, t2, t2i))


def _split_bf16(d):
    hi = d.astype(BF16)
    return hi, (d - hi.astype(F32)).astype(BF16)


def _hi_lo_rows(t):
    hi, lo = _split_bf16(t)
    return jnp.concatenate([hi, lo], axis=-2)


def _ld_lanes(ref, rows):
    return jnp.concatenate([ref[i, rows, :] for i in range(ref.shape[0])], axis=1)


def _st_lanes(ref, rows, val):
    for i in range(ref.shape[0]):
        ref[i, rows, :] = val[:, i * LANES:(i + 1) * LANES]


def _dot3(t, d, m):
    d_hi, d_lo = _split_bf16(d)
    y = jnp.dot(t, d_hi, preferred_element_type=F32)
    return y[:m] + y[m:] + jnp.dot(t[:m], d_lo, preferred_element_type=F32)


def _fft_steps(N1, N2):
    return min(FFT_G, N2), min(FFT_KC, N1)


def _filter_fft_body(k_ref, t1f_ref, t2_ref, o_ref, sr, si, *, N1, N2, G, KC):
    s = pl.program_id(2)
    sa = N2 // G

    @pl.when(s < sa)
    def _():
        def it(r, c):
            n2 = s * G + r
            rows = k_ref[pl.ds(pl.multiple_of(r * N1, N1), N1), :]
            a = _dot3(t1f_ref[n2], rows, 2 * N1)
            _st_lanes(sr, pl.ds(n2, N1, stride=N2), a[:N1])
            _st_lanes(si, pl.ds(n2, N1, stride=N2), a[N1:])
            return c
        lax.fori_loop(0, G, it, 0)

    @pl.when(s >= sa)
    def _():
        t2 = t2_ref[...]

        def it(kk, c):
            r0 = pl.multiple_of(((s - sa) * KC + kk) * N2, N2)
            q0 = pl.multiple_of(kk * N2, N2)
            x = _dot3(t2, jnp.concatenate([_ld_lanes(sr, pl.ds(r0, N2)), _ld_lanes(si, pl.ds(r0, N2))],
                                          axis=0), 2 * N2)
            o_ref[0, pl.ds(q0, N2), :] = x[:N2]
            o_ref[1, pl.ds(q0, N2), :] = x[N2:]
            return c
        lax.fori_loop(0, KC, it, 0)


def _filter_fft(kern, t1f, t2):
    O, N, C = kern.shape
    N1, N2 = FFT_N1, N // FFT_N1
    G, KC = _fft_steps(N1, N2)
    sa, sb = N2 // G, N1 // KC
    cb = min(HY_CB, C)
    full = lambda shape: pl.BlockSpec(shape, lambda o, j, s: (0,) * len(shape),
                                      pipeline_mode=pl.Buffered(1))
    return pl.pallas_call(
        functools.partial(_filter_fft_body, N1=N1, N2=N2, G=G, KC=KC),
        grid=(O, C // cb, sa + sb),
        in_specs=[pl.BlockSpec((None, G * N1, cb), lambda o, j, s: (o, jnp.minimum(s, sa - 1), j)),
                  full(t1f.shape), full(t2.shape)],
        out_specs=pl.BlockSpec((None, 2, KC * N2, cb),
                               lambda o, j, s: (o, 0, jnp.maximum(s - sa, 0), j)),
        out_shape=jax.ShapeDtypeStruct((O, 2, N, C), F32),
        scratch_shapes=[pltpu.VMEM((cb // LANES, N, LANES), F32)] * 2,
        compiler_params=_cparams("parallel", "parallel", "arbitrary"),
        name="hy_filter_fft",
    )(kern, t1f, t2)


def _long_conv_body(v_ref, g_ref, kf_ref, t1_ref, t1i_ref, t2_ref, t2i_ref, o_ref, sr, si,
                    *, N1, N2, G, KC):
    s = pl.program_id(2)
    sa, sb = N2 // G, N1 // KC
    h = N1 // 2
    cb = v_ref.shape[-1]

    @pl.when(s < sa)
    def _():
        def it(r, c):
            n2 = s * G + r
            q = pl.ds(pl.multiple_of(r * h, h), h)
            d = jnp.concatenate([v_ref[0, q, :], v_ref[1, q, :]], axis=0)
            a = _dot3(t1_ref[n2], d, 2 * N1)
            _st_lanes(sr, pl.ds(n2, N1, stride=N2), a[:N1])
            _st_lanes(si, pl.ds(n2, N1, stride=N2), a[N1:])
            return c
        lax.fori_loop(0, G, it, 0)

    @pl.when(jnp.logical_and(s >= sa, s < sa + sb))
    def _():
        t2, t2i = t2_ref[...], t2i_ref[...]

        def it(kk, c):
            r0 = pl.multiple_of(((s - sa) * KC + kk) * N2, N2)
            q0 = pl.multiple_of(kk * N2, N2)
            x = _dot3(t2, jnp.concatenate([_ld_lanes(sr, pl.ds(r0, N2)), _ld_lanes(si, pl.ds(r0, N2))],
                                          axis=0), 2 * N2)
            xr, xi = x[:N2], x[N2:]
            kr = kf_ref[0, pl.ds(q0, N2), :]
            ki = kf_ref[1, pl.ds(q0, N2), :]
            p = jnp.concatenate([xr * kr - xi * ki, xr * ki + xi * kr], axis=0)
            y = _dot3(t2i, p, 2 * N2)
            _st_lanes(sr, pl.ds(r0, N2), y[:N2])
            _st_lanes(si, pl.ds(r0, N2), y[N2:])
            return c
        lax.fori_loop(0, KC, it, 0)

    @pl.when(s >= sa + sb)
    def _():
        def it(r, c):
            n2 = (s - sa - sb) * G + r
            q = pl.ds(pl.multiple_of(r * h, h), h)
            d = jnp.concatenate([_ld_lanes(sr, pl.ds(n2, N1, stride=N2)),
                                 _ld_lanes(si, pl.ds(n2, N1, stride=N2))], axis=0)
            y = _dot3(t1i_ref[n2], d, N1)
            for b in range(2):
                o_ref[b, q, :] = g_ref[b, q, :] * y[b * h:(b + 1) * h]
            return c
        lax.fori_loop(0, G, it, 0)


def _long_conv(v_arr, v_off, g_arr, g_off, kfreq, order, tables, C):
    B, L, _ = v_arr.shape
    t1, t1i, _, t2, t2i = tables
    N = 2 * L
    N1, N2 = FFT_N1, N // FFT_N1
    G, KC = _fft_steps(N1, N2)
    sa, sb = N2 // G, N1 // KC
    h = N1 // 2
    cb = min(HY_CB, C)
    nj = C // cb
    full = lambda shape: pl.BlockSpec(shape, lambda j, p, s: (0,) * len(shape),
                                      pipeline_mode=pl.Buffered(1))
    last = lambda s: jnp.clip(s - sa - sb, 0, sa - 1)
    return pl.pallas_call(
        functools.partial(_long_conv_body, N1=N1, N2=N2, G=G, KC=KC),
        grid=(nj, B // 2, 2 * sa + sb),
        in_specs=[pl.BlockSpec((2, G * h, cb), lambda j, p, s: (p, jnp.minimum(s, sa - 1), v_off * nj + j)),
                  pl.BlockSpec((2, G * h, cb), lambda j, p, s: (p, last(s), g_off * nj + j)),
                  pl.BlockSpec((None, 2, KC * N2, cb),
                               lambda j, p, s: (order, 0, jnp.clip(s - sa, 0, sb - 1), j)),
                  full(t1.shape), full(t1i.shape), full(t2.shape), full(t2i.shape)],
        out_specs=pl.BlockSpec((2, G * h, cb), lambda j, p, s: (p, last(s), j)),
        out_shape=jax.ShapeDtypeStruct((B, L, C), F32),
        scratch_shapes=[pltpu.VMEM((cb // LANES, N, LANES), F32)] * 2,
        compiler_params=_cparams("parallel", "arbitrary", "arbitrary"),
        name="hy_long_conv",
    )(v_arr, g_arr, kfreq, t1, t1i, t2, t2i)


def _log_sigmoid(x):
    return jnp.minimum(x, 0.0) - jnp.log1p(jnp.exp(-jnp.abs(x)))


def _mlstm_body(*refs, rev, T, scale, final):
    if final:
        (q_ref, k_ref, v_ref, gc_ref, gr_ref, bc_ref, br_ref, hprev_ref, o_ref, ng_ref,
         out_ref, c_s, n_s, m_s) = refs
    else:
        q_ref, k_ref, v_ref, gc_ref, gr_ref, bc_ref, br_ref, out_ref, c_s, n_s, m_s = refs

    @pl.when(pl.program_id(2) == 0)
    def _():
        c_s[...] = jnp.zeros_like(c_s)
        n_s[...] = jnp.zeros_like(n_s)
        m_s[...] = jnp.zeros_like(m_s)

    gi = 2 if rev else 0
    gc = gc_ref[...] + bc_ref[...]
    gr = gr_ref[...] + br_ref[...]
    li_c, lf_c = gc[:, gi:gi + 1], _log_sigmoid(gc[:, gi + 1:gi + 2])
    li_r, lf_r = gr[gi:gi + 1, :], _log_sigmoid(gr[gi + 1:gi + 2, :])
    row = lax.broadcasted_iota(jnp.int32, (T, T), 0)
    col = lax.broadcasted_iota(jnp.int32, (T, T), 1)
    valid = (col >= row) if rev else (col <= row)
    valid_t = (row >= col) if rev else (row <= col)
    b_c = jnp.sum(jnp.where(valid, lf_r, 0.0), axis=1, keepdims=True)
    b_r = jnp.sum(jnp.where(valid_t, lf_c, 0.0), axis=0, keepdims=True)
    m = m_s[...]
    d = jnp.where(valid, b_c - b_r + li_r, -jnp.inf)
    inter = b_c + m
    m_t = jnp.maximum(inter, jnp.max(d, axis=1, keepdims=True))
    q, k, v = q_ref[...], k_ref[...], v_ref[...]
    qk = lax.dot_general(q, k, (((1,), (1,)), ((), ())), preferred_element_type=F32)
    s = qk * scale * jnp.exp(d - m_t)
    w_inter = jnp.exp(inter - m_t)
    qc = jnp.dot(q, c_s[...].astype(BF16), preferred_element_type=F32) * scale
    num = jnp.dot(s.astype(BF16), v, preferred_element_type=F32) + w_inter * qc
    qn = jnp.sum(q.astype(F32) * n_s[...], axis=1, keepdims=True) * scale
    den = jnp.sum(s, axis=1, keepdims=True) + w_inter * qn
    hout = num / jnp.maximum(jnp.abs(den), jnp.exp(-m_t))

    b_last = b_c[0:1, :] if rev else b_c[T - 1:T, :]
    w_c = b_last - b_c + li_c
    m_new = jnp.maximum(b_last + m, jnp.max(w_c, axis=0, keepdims=True))
    kw = k.astype(F32) * jnp.exp(w_c - m_new)
    decay = jnp.exp(b_last + m - m_new)
    c_s[...] = decay * c_s[...] + lax.dot_general(
        kw.astype(BF16), v, (((0,), (0,)), ((), ())), preferred_element_type=F32)
    n_s[...] = decay * n_s[...] + jnp.sum(kw, axis=0, keepdims=True)
    m_s[...] = m_new

    if final:
        hsum = hout + hprev_ref[...]
        mu = jnp.mean(hsum, axis=1, keepdims=True)
        var = jnp.mean(jnp.square(hsum - mu), axis=1, keepdims=True)
        hn = (hsum - mu) * lax.rsqrt(var + LN_EPS) * ng_ref[...]
        out_ref[...] = (jax.nn.sigmoid(o_ref[...].astype(F32)) * hn).astype(out_ref.dtype)
    else:
        out_ref[...] = hout


def _mlstm(zq, gates_c, gates_r, bias_c, bias_r, B, L, dk, dv, rev, hprev=None, norm_g=None):
    H = ML_HEADS
    T = min(ML_CHUNK, L)
    nc = L // T
    final = hprev is not None
    cidx = (lambda c: nc - 1 - c) if rev else (lambda c: c)
    kq, kv = H * dk // dk, (2 * H * dk) // dv
    in_specs = [
        pl.BlockSpec((T, dk), lambda b, h, c: (b * nc + cidx(c), h)),
        pl.BlockSpec((T, dk), lambda b, h, c: (b * nc + cidx(c), kq + h)),
        pl.BlockSpec((T, dv), lambda b, h, c: (b * nc + cidx(c), kv + h)),
        pl.BlockSpec((None, None, T, 4), lambda b, h, c: (b, h, cidx(c), 0)),
        pl.BlockSpec((None, None, 4, T), lambda b, h, c: (b, h, 0, cidx(c))),
        pl.BlockSpec((None, 1, 4), lambda b, h, c: (h, 0, 0)),
        pl.BlockSpec((None, 4, 1), lambda b, h, c: (h, 0, 0)),
    ]
    args = [zq, zq, zq, gates_c, gates_r, bias_c, bias_r]
    if final:
        in_specs += [
            pl.BlockSpec((T, dv), lambda b, h, c: (b * nc + cidx(c), h)),
            pl.BlockSpec((T, dv), lambda b, h, c: (b * nc + cidx(c), kv + H + h)),
            pl.BlockSpec((1, dv), lambda b, h, c: (0, h)),
        ]
        args += [hprev, zq, norm_g]
    return pl.pallas_call(
        functools.partial(_mlstm_body, rev=rev, T=T, scale=dk ** -0.5, final=final),
        grid=(B, H, nc),
        in_specs=in_specs,
        out_specs=pl.BlockSpec((T, dv), lambda b, h, c: (b * nc + cidx(c), h)),
        out_shape=jax.ShapeDtypeStruct((B * L, H * dv), BF16 if final else F32),
        scratch_shapes=[pltpu.VMEM((dk, dv), F32), pltpu.VMEM((1, dk), F32), pltpu.VMEM((1, 1), F32)],
        compiler_params=_cparams("parallel", "parallel", "arbitrary"),
        name="mlstm_bwd" if rev else "mlstm_fwd",
    )(*args)


def _merge_body(yh_ref, ym_ref, ph_ref, pm_ref, gh_ref, gm_ref, o_ref):
    a = jnp.dot(yh_ref[...], ph_ref[...], preferred_element_type=F32)
    b = jnp.dot(ym_ref[...], pm_ref[...], preferred_element_type=F32)
    o_ref[...] = (jax.nn.sigmoid(gh_ref[...].astype(F32)) * a
                  + jax.nn.sigmoid(gm_ref[...].astype(F32)) * b).astype(o_ref.dtype)


def _merge(y_hy, y_ml, p_hy, p_ml, gates, tm=512, tn=512):
    n, kh = y_hy.shape
    km = y_ml.shape[1]
    d = p_hy.shape[1]
    nj = d // tn
    return pl.pallas_call(
        _merge_body,
        grid=(n // tm, nj),
        in_specs=[pl.BlockSpec((tm, kh), lambda i, j: (i, 0)),
                  pl.BlockSpec((tm, km), lambda i, j: (i, 0)),
                  pl.BlockSpec((kh, tn), lambda i, j: (0, j)),
                  pl.BlockSpec((km, tn), lambda i, j: (0, j)),
                  pl.BlockSpec((tm, tn), lambda i, j: (i, j)),
                  pl.BlockSpec((tm, tn), lambda i, j: (i, nj + j))],
        out_specs=pl.BlockSpec((tm, tn), lambda i, j: (i, j)),
        out_shape=jax.ShapeDtypeStruct((n, d), BF16),
        compiler_params=_cparams("parallel", "arbitrary"),
        name="gated_merge",
    )(y_hy, y_ml, p_hy, p_ml, gates, gates)


def _layer_norm(x, g, b):
    mu = jnp.mean(x, axis=-1, keepdims=True)
    var = jnp.mean(jnp.square(x - mu), axis=-1, keepdims=True)
    return (x - mu) * lax.rsqrt(var + LN_EPS) * g + b


def _proj_ln_body(a_ref, w_ref, x_ref, g_ref, b_ref, o_ref, acc):
    k = pl.program_id(1)

    @pl.when(k == 0)
    def _():
        acc[...] = jnp.zeros_like(acc)

    acc[...] += jnp.dot(a_ref[...], w_ref[...], preferred_element_type=F32)

    @pl.when(k == pl.num_programs(1) - 1)
    def _():
        o_ref[...] = _layer_norm(DEEPNORM_ALPHA * x_ref[...] + acc[...], g_ref[...], b_ref[...])


def _proj_ln(a, w, x, g, b, tm=512, tk=512):
    n, kd = a.shape
    d = w.shape[1]
    return pl.pallas_call(
        _proj_ln_body,
        grid=(n // tm, kd // tk),
        in_specs=[pl.BlockSpec((tm, tk), lambda i, k: (i, k)),
                  pl.BlockSpec((tk, d), lambda i, k: (k, 0)),
                  pl.BlockSpec((tm, d), lambda i, k: (i, 0), pipeline_mode=pl.Buffered(1)),
                  pl.BlockSpec((1, d), lambda i, k: (0, 0)),
                  pl.BlockSpec((1, d), lambda i, k: (0, 0))],
        out_specs=pl.BlockSpec((tm, d), lambda i, k: (i, 0)),
        out_shape=jax.ShapeDtypeStruct((n, d), F32),
        scratch_shapes=[pltpu.VMEM((tm, d), F32)],
        compiler_params=_cparams("parallel", "arbitrary"),
        name="out_proj_ln",
    )(a, w, x, g.reshape(1, d), b.reshape(1, d))


def _router_body(x_ref, w_ref, b_ref, wout_ref, eout_ref, cnt_ref):
    G, PG = MOE_GROUPS, MOE_PER_GROUP

    @pl.when(pl.program_id(0) == 0)
    def _():
        cnt_ref[...] = jnp.zeros_like(cnt_ref)

    logits = jnp.dot(x_ref[...], w_ref[...], precision=HIGHEST, preferred_element_type=F32) + b_ref[...]
    lane = lax.broadcasted_iota(jnp.int32, logits.shape, 1)
    ninf = -jnp.inf
    first = lambda mask: jnp.min(jnp.where(mask, lane, 2 * LANES), axis=1, keepdims=True)
    lg1 = jnp.where(lane < G, logits, ninf)
    m1 = jnp.max(lg1, axis=1, keepdims=True)
    g_sel = first(lg1 == m1)
    p_group = 1.0 / jnp.sum(jnp.exp(lg1 - m1), axis=1, keepdims=True)
    lo = G + g_sel * PG
    in_grp = jnp.logical_and(lane >= lo, lane < lo + PG)
    lg2 = jnp.where(in_grp, logits, ninf)
    m2 = jnp.max(lg2, axis=1, keepdims=True)
    e2 = jnp.exp(lg2 - m2)
    p2 = jnp.where(in_grp, e2 / jnp.sum(e2, axis=1, keepdims=True), -1.0)
    t1 = jnp.max(p2, axis=1, keepdims=True)
    j1 = first(p2 == t1)
    p2b = jnp.where(lane == j1, -1.0, p2)
    t2 = jnp.max(p2b, axis=1, keepdims=True)
    j2 = first(p2b == t2)
    tot = t1 + t2
    oh1, oh2 = lane == j1, lane == j2
    ohs = jnp.where(jnp.logical_or(oh1, oh2), 1.0, 0.0)
    tm = ohs.shape[0]
    earlier = (lax.broadcasted_iota(jnp.int32, (tm, tm), 1)
               < lax.broadcasted_iota(jnp.int32, (tm, tm), 0))
    base = cnt_ref[...] + jnp.dot(jnp.where(earlier, 1.0, 0.0).astype(BF16), ohs.astype(BF16),
                                  preferred_element_type=F32)
    r1 = jnp.sum(jnp.where(oh1, base, 0.0), axis=1, keepdims=True).astype(jnp.int32)
    r2 = jnp.sum(jnp.where(oh2, base, 0.0), axis=1, keepdims=True).astype(jnp.int32)
    cnt_ref[...] += jnp.sum(ohs, axis=0, keepdims=True)
    wout_ref[...] = jnp.where(lane == 0, p_group * (t1 / tot),
                              jnp.where(lane == 1, p_group * (t2 / tot), 0.0))
    eout_ref[...] = jnp.where(lane == 0, j1 - G, jnp.where(lane == 1, j2 - G,
                              jnp.where(lane == 2, r1, jnp.where(lane == 3, r2, 0))))


def _router(x, router_w1, router_b1, router_w2, router_b2, tm=256):
    n, d = x.shape
    ncol = MOE_GROUPS + MOE_GROUPS * MOE_PER_GROUP
    w = jnp.zeros((d, LANES), F32).at[:, :ncol].set(jnp.concatenate([router_w1, router_w2], axis=1))
    b = jnp.zeros((1, LANES), F32).at[0, :ncol].set(jnp.concatenate([router_b1, router_b2]))
    return pl.pallas_call(
        _router_body,
        grid=(n // tm,),
        in_specs=[pl.BlockSpec((tm, d), lambda i: (i, 0)),
                  pl.BlockSpec((d, LANES), lambda i: (0, 0)),
                  pl.BlockSpec((1, LANES), lambda i: (0, 0))],
        out_specs=[pl.BlockSpec((tm, LANES), lambda i: (i, 0)),
                   pl.BlockSpec((tm, LANES), lambda i: (i, 0)),
                   pl.BlockSpec((1, LANES), lambda i: (0, 0))],
        out_shape=[jax.ShapeDtypeStruct((n, LANES), F32), jax.ShapeDtypeStruct((n, LANES), jnp.int32),
                   jax.ShapeDtypeStruct((1, LANES), F32)],
        compiler_params=_cparams("arbitrary"),
        name="moe_router",
    )(x, w, b)


def _row_copy(src_hbm, row, dst_vmem, r, sem):
    return pltpu.make_async_copy(src_hbm.at[pl.ds(row, 1), :], dst_vmem.at[pl.ds(r, 1), :], sem)


def _expert_body(tok_ref, blk_e_ref, nused_ref, x_hbm, w1_ref, w3_ref, w2_ref, o_ref, xbuf, sem, *, bm):
    del blk_e_ref
    i = pl.program_id(0)
    n_used = nused_ref[0]

    def gather(blk, slot):
        def issue(r, c):
            _row_copy(x_hbm, tok_ref[blk * bm + r], xbuf.at[slot], r, sem.at[slot]).start()
            return c
        lax.fori_loop(0, bm, issue, 0)

    @pl.when(i == 0)
    def _():
        gather(0, 0)

    @pl.when(i + 1 < n_used)
    def _():
        gather(i + 1, (i + 1) % 2)

    @pl.when(i < n_used)
    def _():
        slot = i % 2

        def wait(r, c):
            _row_copy(x_hbm, 0, xbuf.at[slot], r, sem.at[slot]).wait()
            return c
        lax.fori_loop(0, bm, wait, 0)
        xb = xbuf[slot].astype(BF16)
        a = jnp.dot(xb, w1_ref[...], preferred_element_type=F32)
        b = jnp.dot(xb, w3_ref[...], preferred_element_type=F32)
        hb = (a * jax.nn.sigmoid(a)) * b
        o_ref[...] = jnp.dot(hb.astype(BF16), w2_ref[...], preferred_element_type=F32)

    @pl.when(i >= n_used)
    def _():
        o_ref[...] = jnp.zeros_like(o_ref)


def _experts(x, slot_tok, blk_e, n_used, w1, w3, w2, bm):
    n, d = x.shape
    e, _, hd = w1.shape
    n_blocks = blk_e.shape[0]
    grid_spec = pltpu.PrefetchScalarGridSpec(
        num_scalar_prefetch=3,
        grid=(n_blocks,),
        in_specs=[pl.BlockSpec(memory_space=pl.ANY),
                  pl.BlockSpec((None, d, hd), lambda i, tok, be, nu: (be[i], 0, 0)),
                  pl.BlockSpec((None, d, hd), lambda i, tok, be, nu: (be[i], 0, 0)),
                  pl.BlockSpec((None, hd, d), lambda i, tok, be, nu: (be[i], 0, 0))],
        out_specs=pl.BlockSpec((bm, d), lambda i, tok, be, nu: (i, 0)),
        scratch_shapes=[pltpu.VMEM((2, bm, d), F32), pltpu.SemaphoreType.DMA((2,))],
    )
    return pl.pallas_call(
        functools.partial(_expert_body, bm=bm),
        grid_spec=grid_spec,
        out_shape=jax.ShapeDtypeStruct((n_blocks * bm, d), F32),
        compiler_params=_cparams("arbitrary"),
        name="moe_experts",
    )(slot_tok, blk_e, n_used, x, w1, w3, w2)


def _combine_body(slot_ref, y_hbm, w_ref, x_ref, g_ref, b_ref, o_ref, buf, sem, *, tb):
    i = pl.program_id(0)

    def issue(r, c):
        for kk in range(MOE_TOPK):
            _row_copy(y_hbm, slot_ref[(i * tb + r) * MOE_TOPK + kk], buf.at[kk], r, sem).start()
        return c
    lax.fori_loop(0, tb, issue, 0)

    def wait(r, c):
        for kk in range(MOE_TOPK):
            _row_copy(y_hbm, 0, buf.at[kk], r, sem).wait()
        return c
    lax.fori_loop(0, tb, wait, 0)
    w = w_ref[...]
    y = w[:, 0:1] * buf[0] + w[:, 1:2] * buf[1]
    o_ref[...] = _layer_norm(DEEPNORM_ALPHA * x_ref[...] + y, g_ref[...], b_ref[...])


def _combine_ln(slot_of, yb, weights, x, g, b, tb):
    n, d = x.shape
    grid_spec = pltpu.PrefetchScalarGridSpec(
        num_scalar_prefetch=1,
        grid=(n // tb,),
        in_specs=[pl.BlockSpec(memory_space=pl.ANY),
                  pl.BlockSpec((tb, LANES), lambda i, s: (i, 0)),
                  pl.BlockSpec((tb, d), lambda i, s: (i, 0)),
                  pl.BlockSpec((1, d), lambda i, s: (0, 0)),
                  pl.BlockSpec((1, d), lambda i, s: (0, 0))],
        out_specs=pl.BlockSpec((tb, d), lambda i, s: (i, 0)),
        scratch_shapes=[pltpu.VMEM((MOE_TOPK, tb, d), F32), pltpu.SemaphoreType.DMA(())],
    )
    return pl.pallas_call(
        functools.partial(_combine_body, tb=tb),
        grid_spec=grid_spec,
        out_shape=jax.ShapeDtypeStruct((n, d), F32),
        compiler_params=_cparams("arbitrary"),
        name="moe_combine_ln",
    )(slot_of, yb, weights, x, g.reshape(1, d), b.reshape(1, d))


def _moe(h1, router_w1, router_b1, router_w2, router_b2, exp_w1, exp_w3, exp_w2, ln_g, ln_b):
    n, d = h1.shape
    e = exp_w1.shape[0]
    bm = MOE_BM
    weights, ids, cnt = _router(h1, router_w1, router_b1, router_w2, router_b2)
    m = n * MOE_TOPK
    eid_f = ids[:, :MOE_TOPK].reshape(m)
    rank = ids[:, MOE_TOPK:2 * MOE_TOPK].reshape(m)
    counts = cnt[0, MOE_GROUPS:MOE_GROUPS + e].astype(jnp.int32)
    nblk_e = (counts + bm - 1) // bm
    bend = jnp.cumsum(nblk_e)
    pstart = (bend - nblk_e) * bm
    slot_of = (pstart[eid_f] + rank).astype(jnp.int32)
    n_blocks = -(-m // bm) + e
    tok_f = jnp.arange(m, dtype=jnp.int32) // MOE_TOPK
    slot_tok = jnp.zeros((n_blocks * bm,), jnp.int32).at[slot_of].set(tok_f)
    n_used = bend[-1:].astype(jnp.int32)
    blk = jnp.minimum(jnp.arange(n_blocks, dtype=jnp.int32), n_used[0] - 1)
    blk_e = jnp.minimum(jnp.searchsorted(bend, blk, side='right'), e - 1).astype(jnp.int32)
    yb = _experts(h1, slot_tok, blk_e, n_used, exp_w1.astype(BF16), exp_w3.astype(BF16),
                  exp_w2.astype(BF16), bm)
    return _combine_ln(slot_of, yb, weights, h1, ln_g, ln_b, min(MOE_TB, n))


def kernel(x, w_in, hy_conv_w, hy_conv_b, hy_f_w1, hy_f_b1, hy_f_fr1, hy_f_w2, hy_f_b2, hy_f_fr2,
           hy_f_w3, hy_f_b3, hy_f_fr3, hy_f_wout, hy_bias, ml_gate_bias, ml_norm_g, p_hy, p_ml, w_out,
           ln1_g, ln1_b, router_w1, router_b1, router_w2, router_b2, exp_w1, exp_w3, exp_w2,
           ln2_g, ln2_b):
    B, L, D = x.shape
    N = B * L
    C = D // 2
    H = ML_HEADS
    dv = C // H
    dk = dv // 2
    col_q = (HY_ORDER + 1) * C
    col_if = col_q + 2 * H * dk + 2 * C
    col_gate = col_if + 4 * H

    xf = x.reshape(N, D)
    xb = xf.astype(BF16)
    w_hy = w_in[:, :col_q].astype(BF16)
    w_ml = w_in[:, col_q:col_if].astype(BF16)
    w_if = jnp.zeros((D, LANES), BF16).at[:, :4 * H].set(w_in[:, col_if:col_gate].astype(BF16))
    w_gt = w_in[:, col_gate:].astype(BF16)

    z_hy = _matmul(xb, w_hy, F32, 1024, 512).reshape(B, L, col_q)
    z_ml = _matmul(xb, w_ml, BF16, 1024, 512)
    z_if = _matmul(xb, w_if, F32, 1024, LANES)
    z_gt = _matmul(xb, w_gt, BF16, 1024, 512)

    tables = _dft_tables(L)
    kern = _hyena_filters(L, C, hy_f_w1, hy_f_b1, hy_f_fr1, hy_f_w2, hy_f_b2, hy_f_fr2,
                          hy_f_w3, hy_f_b3, hy_f_fr3, hy_f_wout, hy_bias)
    n2 = 2 * L // FFT_N1
    kfreq = _filter_fft(_row_permute(kern, FFT_N1, n2, F32), tables[2], tables[3])
    u = _short_conv(z_hy, hy_conv_w, hy_conv_b, n2)
    v1 = _long_conv(u, 2, u, 0, kfreq, 0, tables, C)
    y_hy = _long_conv(v1, 0, u, 1, kfreq, 1, tables, C)
    y_hy = _row_permute(y_hy, n2, L // n2, BF16).reshape(N, C)

    g = z_if[:, :4 * H].reshape(B, L, 4, H)
    gates_c = g.transpose(0, 3, 1, 2)
    gates_r = g.transpose(0, 3, 2, 1)
    bias_c = ml_gate_bias.T.reshape(H, 1, 4)
    bias_r = ml_gate_bias.T.reshape(H, 4, 1)
    h_fwd = _mlstm(z_ml, gates_c, gates_r, bias_c, bias_r, B, L, dk, dv, rev=False)
    y_ml = _mlstm(z_ml, gates_c, gates_r, bias_c, bias_r, B, L, dk, dv, rev=True,
                  hprev=h_fwd, norm_g=ml_norm_g.reshape(1, C))

    merged = _merge(y_hy, y_ml, p_hy.astype(BF16), p_ml.astype(BF16), z_gt)
    h1 = _proj_ln(merged, w_out.astype(BF16), xf, ln1_g, ln1_b)
    out = _moe(h1, router_w1, router_b1, router_w2, router_b2, exp_w1, exp_w3, exp_w2, ln2_g, ln2_b)
    return out.reshape(B, L, D)
```

```python
import functools
import math

import jax
import jax.numpy as jnp
from jax import lax
from jax.experimental import pallas as pl
from jax.experimental.pallas import tpu as pltpu

F32 = jnp.float32
BF16 = jnp.bfloat16
HIGHEST = lax.Precision.HIGHEST

VMEM_LIMIT_BYTES = 56 * 1024 * 1024
LANES = 128

HY_ORDER = 2
HY_SHORT = 3
HY_POS_EMB = 33
HY_DECAY_TARGET = 1e-2
HY_FAST_DECAY = 0.3
HY_SLOW_DECAY = 1.5
HY_MOD_SHIFT = 0.05
ML_HEADS = 8
MOE_GROUPS = 8
MOE_PER_GROUP = 8
MOE_TOPK = 2
DEPTH = 1
DEEPNORM_ALPHA = (2.0 * DEPTH) ** 0.25
LN_EPS = 1e-5

FFT_N1 = 64
HY_CB = 256
FFT_G = 16
FFT_KC = 8
FFT_UNROLL_OUTER = 4
FFT_UNROLL_INNER = 4
ML_CHUNK = 256
MOE_BM = 128
MOE_TB = 256


def _cparams(*sem):
    return pltpu.CompilerParams(dimension_semantics=sem, vmem_limit_bytes=VMEM_LIMIT_BYTES)


def _mm_body(a_ref, b_ref, o_ref):
    o_ref[...] = jnp.dot(a_ref[...], b_ref[...], preferred_element_type=F32).astype(o_ref.dtype)


def _matmul(a, b, out_dtype, tm, tn):
    m, k = a.shape
    _, n = b.shape
    assert m % tm == 0 and n % tn == 0
    return pl.pallas_call(
        _mm_body,
        grid=(m // tm, n // tn),
        in_specs=[pl.BlockSpec((tm, k), lambda i, j: (i, 0)),
                  pl.BlockSpec((k, tn), lambda i, j: (0, j))],
        out_specs=pl.BlockSpec((tm, tn), lambda i, j: (i, j)),
        out_shape=jax.ShapeDtypeStruct((m, n), out_dtype),
        compiler_params=_cparams("parallel", "arbitrary"),
        name="proj_matmul",
    )(a, b)


def _permute_pitch(outer):
    return outer + 8


def _permute_rows(get_rows, tmp_ref, dst_ref, inner, outer):
    pitch = _permute_pitch(outer)
    for b in range(inner):
        tmp_ref[0, b * pitch:b * pitch + outer, :] = get_rows(b * outer, (b + 1) * outer)

    def it(a, c):
        rows = tmp_ref[0, pl.ds(a, inner, stride=pitch), :]
        dst_ref[0, pl.ds(pl.multiple_of(a * inner, inner), inner), :] = rows.astype(dst_ref.dtype)
        return c
    lax.fori_loop(0, outer, it, 0, unroll=4)


def _short_conv_body(z_ref, w_ref, b_ref, o_ref, tmp, *, n2):
    z = z_ref[0]
    L = z.shape[0]
    row = lax.broadcasted_iota(jnp.int32, z.shape, 0)
    prev = jnp.where(row == 0, 0.0, pltpu.roll(z, 1, 0))
    nxt = jnp.where(row == L - 1, 0.0, pltpu.roll(z, L - 1, 0))
    u = b_ref[...] + prev * w_ref[0:1, :] + z * w_ref[1:2, :] + nxt * w_ref[2:3, :]
    _permute_rows(lambda lo, hi: u[lo:hi], tmp, o_ref, L // n2, n2)


def _short_conv(z, w, b, n2, cb=LANES):
    B, L, C = z.shape
    return pl.pallas_call(
        functools.partial(_short_conv_body, n2=n2),
        grid=(B, C // cb),
        in_specs=[pl.BlockSpec((1, L, cb), lambda i, j: (i, 0, j)),
                  pl.BlockSpec((HY_SHORT, cb), lambda i, j: (0, j)),
                  pl.BlockSpec((1, cb), lambda i, j: (0, j))],
        out_specs=pl.BlockSpec((1, L, cb), lambda i, j: (i, 0, j)),
        out_shape=jax.ShapeDtypeStruct((B, L, C), F32),
        scratch_shapes=[pltpu.VMEM((1, (L // n2) * _permute_pitch(n2), cb), F32)],
        compiler_params=_cparams("parallel", "parallel"),
        name="hy_short_conv",
    )(z, w, b.reshape(1, C))


def _row_permute_body(x_ref, o_ref, tmp, *, inner, outer):
    _permute_rows(lambda lo, hi: x_ref[0, lo:hi, :], tmp, o_ref, inner, outer)


def _row_permute(x, inner, outer, out_dtype, cb=LANES):
    A, R, C = x.shape
    assert R == inner * outer
    cb = min(cb, C)
    return pl.pallas_call(
        functools.partial(_row_permute_body, inner=inner, outer=outer),
        grid=(A, C // cb),
        in_specs=[pl.BlockSpec((1, R, cb), lambda i, j: (i, 0, j))],
        out_specs=pl.BlockSpec((1, R, cb), lambda i, j: (i, 0, j)),
        out_shape=jax.ShapeDtypeStruct((A, R, C), out_dtype),
        scratch_shapes=[pltpu.VMEM((1, inner * _permute_pitch(outer), cb), F32)],
        compiler_params=_cparams("parallel", "parallel"),
        name="hy_row_permute",
    )(x)


def _filter_hidden_body(w1_ref, b1_ref, fr1_ref, w2_ref, b2_ref, fr2_ref, w3_ref, b3_ref, fr3_ref,
                        o_ref, *, L, rows):
    i = pl.program_id(0)
    n = i * rows + lax.broadcasted_iota(jnp.int32, (rows, 1), 0)
    pos = jnp.where(n < L, n, 2 * L - n).astype(F32)
    t = pos / (L - 1.0)
    w = (2.0 * math.pi / L) * pos
    lane = lax.broadcasted_iota(jnp.int32, (1, LANES), 1)
    bands = (HY_POS_EMB - 1) // 2
    band = jnp.where(lane <= bands, lane - 1, lane - 1 - bands).astype(F32)
    freq = 1e-4 + band * ((bands - 1 - 1e-4) / (bands - 1))
    ang = w * freq
    feats = jnp.where(lane == 0, t,
                      jnp.where(lane <= bands, jnp.cos(ang),
                                jnp.where(lane <= 2 * bands, -jnp.sin(ang), 0.0)))
    h = jnp.sin(fr1_ref[...] * (jnp.dot(feats, w1_ref[...], precision=HIGHEST,
                                        preferred_element_type=F32) + b1_ref[...]))
    h = jnp.sin(fr2_ref[...] * (jnp.dot(h, w2_ref[...], precision=HIGHEST,
                                        preferred_element_type=F32) + b2_ref[...]))
    h = jnp.sin(fr3_ref[...] * (jnp.dot(h, w3_ref[...], precision=HIGHEST,
                                        preferred_element_type=F32) + b3_ref[...]))
    o_ref[...] = h


def _filter_out_body(h_ref, wout_ref, delta_ref, bias_ref, o_ref, *, L):
    d = pl.program_id(1)
    h = jnp.dot(h_ref[...], wout_ref[0], precision=HIGHEST, preferred_element_type=F32)
    r = lax.broadcasted_iota(jnp.int32, (L, 1), 0)
    pos = jnp.where(d == 0, r, L - r).astype(F32)
    t = pos / (L - 1.0)
    window = jnp.exp(-t * delta_ref[...]) + HY_MOD_SHIFT
    first = r == 0
    tap = jnp.where(jnp.logical_and(d == 1, first), 0.0, h * window)
    o_ref[0] = tap + jnp.where(jnp.logical_and(d == 0, first), bias_ref[...], 0.0)


def _hyena_filters(L, C, f_w1, f_b1, f_fr1, f_w2, f_b2, f_fr2, f_w3, f_b3, f_fr3, f_wout, bias, cb=512):
    fh = f_w2.shape[0]
    rows = 1024 if (2 * L) % 1024 == 0 else 2 * L
    w1p = jnp.zeros((LANES, fh), F32).at[:HY_POS_EMB].set(f_w1)
    vec = lambda a: a.reshape(1, fh)
    full = lambda shape: pl.BlockSpec(shape, lambda i: (0,) * len(shape))
    hid = pl.pallas_call(
        functools.partial(_filter_hidden_body, L=L, rows=rows),
        grid=(2 * L // rows,),
        in_specs=[full((LANES, fh)), full((1, fh)), full((1, fh)),
                  full((fh, fh)), full((1, fh)), full((1, fh)),
                  full((fh, fh)), full((1, fh)), full((1, fh))],
        out_specs=pl.BlockSpec((rows, fh), lambda i: (i, 0)),
        out_shape=jax.ShapeDtypeStruct((2 * L, fh), F32),
        compiler_params=_cparams("parallel"),
        name="hy_filter_hidden",
    )(w1p, vec(f_b1), vec(f_fr1), f_w2, vec(f_b2), vec(f_fr2), f_w3, vec(f_b3), vec(f_fr3))
    wout = f_wout.reshape(fh, 2, HY_ORDER, C).transpose(1, 2, 0, 3).reshape(2 * HY_ORDER, fh, C)
    deltas = jnp.abs(jnp.linspace(math.log(HY_DECAY_TARGET) / HY_SLOW_DECAY,
                                  math.log(HY_DECAY_TARGET) / HY_FAST_DECAY, C, dtype=F32))
    cb = min(cb, C)
    return pl.pallas_call(
        functools.partial(_filter_out_body, L=L),
        grid=(HY_ORDER, 2, C // cb),
        in_specs=[pl.BlockSpec((L, fh), lambda o, d, j: (d, 0)),
                  pl.BlockSpec((1, fh, cb), lambda o, d, j: (d * HY_ORDER + o, 0, j)),
                  pl.BlockSpec((1, cb), lambda o, d, j: (0, j)),
                  pl.BlockSpec((None, 1, cb), lambda o, d, j: (o, 0, j))],
        out_specs=pl.BlockSpec((1, L, cb), lambda o, d, j: (o, d, j)),
        out_shape=jax.ShapeDtypeStruct((HY_ORDER, 2 * L, C), F32),
        compiler_params=_cparams("parallel", "parallel", "parallel"),
        name="hy_filter_out",
    )(hid, wout, deltas.reshape(1, C), bias.reshape(HY_ORDER, 1, C))


def _dft_tables(L):
    N = 2 * L
    N1 = FFT_N1
    N2 = N // N1
    h = N1 // 2
    n2 = jnp.arange(N2, dtype=jnp.int32)[:, None, None]
    k1 = jnp.arange(N1, dtype=jnp.int32)[None, :, None]
    n1 = jnp.arange(N1, dtype=jnp.int32)[None, None, :]
    ph = (k1 * (N2 * n1 + n2)) % N
    ang = ph.astype(F32) * (-2.0 * math.pi / N)
    mr, mi = jnp.cos(ang), jnp.sin(ang)
    mrp, mip = mr[:, :, :h], mi[:, :, :h]
    t1 = jnp.concatenate([jnp.concatenate([mrp, -mip], axis=2),
                          jnp.concatenate([mip, mrp], axis=2)], axis=1)
    mrt, mit = jnp.swapaxes(mrp, 1, 2) / N, jnp.swapaxes(mip, 1, 2) / N
    t1i = jnp.concatenate([jnp.concatenate([mrt, mit], axis=2),
                           jnp.concatenate([-mit, mrt], axis=2)], axis=1)
    t1f = jnp.concatenate([mr, mi], axis=1)
    a = jnp.arange(N2, dtype=jnp.int32)
    ang2 = ((a[:, None] * a[None, :]) % N2).astype(F32) * (-2.0 * math.pi / N2)
    fr, fi = jnp.cos(ang2), jnp.sin(ang2)
    t2 = jnp.concatenate([jnp.concatenate([fr, -fi], axis=1),
                          jnp.concatenate([fi, fr], axis=1)], axis=0)
    t2i = jnp.concatenate([jnp.concatenate([fr, fi], axis=1),
                           jnp.concatenate([-fi, fr], axis=1)], axis=0)
    conv_tables = tuple(t.astype(BF16) for t in (t1, t1i, t2, t2i))
    filter_tables = (_hi_lo_rows(t1f), _hi_lo_rows(t2))
    return conv_tables, filter_tables


def _split_bf16(d):
    hi = d.astype(BF16)
    return hi, (d - hi.astype(F32)).astype(BF16)


def _hi_lo_rows(t):
    hi, lo = _split_bf16(t)
    return jnp.concatenate([hi, lo], axis=-2)


def _ld_lanes(ref, rows):
    return jnp.concatenate([ref[i, rows, :] for i in range(ref.shape[0])], axis=1)


def _st_lanes(ref, rows, val):
    for i in range(ref.shape[0]):
        ref[i, rows, :] = val[:, i * LANES:(i + 1) * LANES]


def _dot3(t, d, m):
    d_hi, d_lo = _split_bf16(d)
    y = jnp.dot(t, d_hi, preferred_element_type=F32)
    return y[:m] + y[m:] + jnp.dot(t[:m], d_lo, preferred_element_type=F32)


def _dot1(t, d):
    return jnp.dot(t, d.astype(BF16), preferred_element_type=F32)


def _work_pitch(N2):
    return N2 + 8


def _fft_steps(N1, N2):
    return min(FFT_G, N2), min(FFT_KC, N1)


def _filter_fft_body(k_ref, t1f_ref, t2_ref, o_ref, sr, si, *, N1, N2, G, KC):
    s = pl.program_id(2)
    sa = N2 // G
    P = _work_pitch(N2)

    @pl.when(s < sa)
    def _():
        def it(r, c):
            n2 = s * G + r
            rows = k_ref[pl.ds(pl.multiple_of(r * N1, N1), N1), :]
            a = _dot3(t1f_ref[n2], rows, 2 * N1)
            _st_lanes(sr, pl.ds(n2, N1, stride=P), a[:N1])
            _st_lanes(si, pl.ds(n2, N1, stride=P), a[N1:])
            return c
        lax.fori_loop(0, G, it, 0, unroll=FFT_UNROLL_OUTER)

    @pl.when(s >= sa)
    def _():
        t2 = t2_ref[...]

        def it(kk, c):
            r0 = pl.multiple_of(((s - sa) * KC + kk) * P, 8)
            q0 = pl.multiple_of(kk * N2, N2)
            x = _dot3(t2, jnp.concatenate([_ld_lanes(sr, pl.ds(r0, N2)), _ld_lanes(si, pl.ds(r0, N2))],
                                          axis=0), 2 * N2)
            o_ref[0, pl.ds(q0, N2), :] = x[:N2]
            o_ref[1, pl.ds(q0, N2), :] = x[N2:]
            return c
        lax.fori_loop(0, KC, it, 0, unroll=FFT_UNROLL_INNER)


def _filter_fft(kern, t1f, t2):
    O, N, C = kern.shape
    N1, N2 = FFT_N1, N // FFT_N1
    G, KC = _fft_steps(N1, N2)
    sa, sb = N2 // G, N1 // KC
    cb = min(HY_CB, C)
    full = lambda shape: pl.BlockSpec(shape, lambda o, j, s: (0,) * len(shape),
                                      pipeline_mode=pl.Buffered(1))
    return pl.pallas_call(
        functools.partial(_filter_fft_body, N1=N1, N2=N2, G=G, KC=KC),
        grid=(O, C // cb, sa + sb),
        in_specs=[pl.BlockSpec((None, G * N1, cb), lambda o, j, s: (o, jnp.minimum(s, sa - 1), j)),
                  full(t1f.shape), full(t2.shape)],
        out_specs=pl.BlockSpec((None, 2, KC * N2, cb),
                               lambda o, j, s: (o, 0, jnp.maximum(s - sa, 0), j)),
        out_shape=jax.ShapeDtypeStruct((O, 2, N, C), F32),
        scratch_shapes=[pltpu.VMEM((cb // LANES, N1 * _work_pitch(N2), LANES), F32)] * 2,
        compiler_params=_cparams("parallel", "parallel", "arbitrary"),
        name="hy_filter_fft",
    )(kern, t1f, t2)


def _long_conv_body(v_ref, g_ref, kf_ref, t1_ref, t1i_ref, t2_ref, t2i_ref, o_ref, sr, si,
                    *, N1, N2, G, KC):
    s = pl.program_id(2)
    sa, sb = N2 // G, N1 // KC
    h = N1 // 2
    P = _work_pitch(N2)

    @pl.when(s < sa)
    def _():
        def it(r, c):
            n2 = s * G + r
            q = pl.ds(pl.multiple_of(r * h, h), h)
            d = jnp.concatenate([v_ref[0, q, :], v_ref[1, q, :]], axis=0)
            a = _dot1(t1_ref[n2], d)
            _st_lanes(sr, pl.ds(n2, N1, stride=P), a[:N1])
            _st_lanes(si, pl.ds(n2, N1, stride=P), a[N1:])
            return c
        lax.fori_loop(0, G, it, 0, unroll=FFT_UNROLL_OUTER)

    @pl.when(jnp.logical_and(s >= sa, s < sa + sb))
    def _():
        t2, t2i = t2_ref[...], t2i_ref[...]

        def it(kk, c):
            r0 = pl.multiple_of(((s - sa) * KC + kk) * P, 8)
            q0 = pl.multiple_of(kk * N2, N2)
            x = _dot1(t2, jnp.concatenate([_ld_lanes(sr, pl.ds(r0, N2)), _ld_lanes(si, pl.ds(r0, N2))],
                                          axis=0))
            xr, xi = x[:N2], x[N2:]
            kr = kf_ref[0, pl.ds(q0, N2), :]
            ki = kf_ref[1, pl.ds(q0, N2), :]
            p = jnp.concatenate([xr * kr - xi * ki, xr * ki + xi * kr], axis=0)
            y = _dot1(t2i, p)
            _st_lanes(sr, pl.ds(r0, N2), y[:N2])
            _st_lanes(si, pl.ds(r0, N2), y[N2:])
            return c
        lax.fori_loop(0, KC, it, 0, unroll=FFT_UNROLL_INNER)

    @pl.when(s >= sa + sb)
    def _():
        def it(r, c):
            n2 = (s - sa - sb) * G + r
            q = pl.ds(pl.multiple_of(r * h, h), h)
            d = jnp.concatenate([_ld_lanes(sr, pl.ds(n2, N1, stride=P)),
                                 _ld_lanes(si, pl.ds(n2, N1, stride=P))], axis=0)
            y = _dot1(t1i_ref[n2], d)
            for b in range(2):
                o_ref[b, q, :] = g_ref[b, q, :] * y[b * h:(b + 1) * h]
            return c
        lax.fori_loop(0, G, it, 0, unroll=FFT_UNROLL_OUTER)


def _long_conv(v_arr, v_off, g_arr, g_off, kfreq, order, tables, C):
    B, L, _ = v_arr.shape
    t1, t1i, t2, t2i = tables
    N = 2 * L
    N1, N2 = FFT_N1, N // FFT_N1
    G, KC = _fft_steps(N1, N2)
    sa, sb = N2 // G, N1 // KC
    h = N1 // 2
    cb = min(HY_CB, C)
    nj = C // cb
    full = lambda shape: pl.BlockSpec(shape, lambda j, p, s: (0,) * len(shape),
                                      pipeline_mode=pl.Buffered(1))
    last = lambda s: jnp.clip(s - sa - sb, 0, sa - 1)
    return pl.pallas_call(
        functools.partial(_long_conv_body, N1=N1, N2=N2, G=G, KC=KC),
        grid=(nj, B // 2, 2 * sa + sb),
        in_specs=[pl.BlockSpec((2, G * h, cb), lambda j, p, s: (p, jnp.minimum(s, sa - 1), v_off * nj + j)),
                  pl.BlockSpec((2, G * h, cb), lambda j, p, s: (p, last(s), g_off * nj + j)),
                  pl.BlockSpec((None, 2, KC * N2, cb),
                               lambda j, p, s: (order, 0, jnp.clip(s - sa, 0, sb - 1), j)),
                  full(t1.shape), full(t1i.shape), full(t2.shape), full(t2i.shape)],
        out_specs=pl.BlockSpec((2, G * h, cb), lambda j, p, s: (p, last(s), j)),
        out_shape=jax.ShapeDtypeStruct((B, L, C), F32),
        scratch_shapes=[pltpu.VMEM((cb // LANES, N1 * _work_pitch(N2), LANES), F32)] * 2,
        compiler_params=_cparams("parallel", "arbitrary", "arbitrary"),
        name="hy_long_conv",
    )(v_arr, g_arr, kfreq, t1, t1i, t2, t2i)


def _log_sigmoid(x):
    return jnp.minimum(x, 0.0) - jnp.log1p(jnp.exp(-jnp.abs(x)))


def _mlstm_body(*refs, rev, T, scale, final):
    if final:
        (q_ref, k_ref, v_ref, gc_ref, gr_ref, bc_ref, br_ref, hprev_ref, o_ref, ng_ref,
         out_ref, c_s, n_s, m_s) = refs
    else:
        q_ref, k_ref, v_ref, gc_ref, gr_ref, bc_ref, br_ref, out_ref, c_s, n_s, m_s = refs

    @pl.when(pl.program_id(2) == 0)
    def _():
        c_s[...] = jnp.zeros_like(c_s)
        n_s[...] = jnp.zeros_like(n_s)
        m_s[...] = jnp.zeros_like(m_s)

    gi = 2 if rev else 0
    gc = gc_ref[...] + bc_ref[...]
    gr = gr_ref[...] + br_ref[...]
    li_c, lf_c = gc[:, gi:gi + 1], _log_sigmoid(gc[:, gi + 1:gi + 2])
    li_r, lf_r = gr[gi:gi + 1, :], _log_sigmoid(gr[gi + 1:gi + 2, :])
    row = lax.broadcasted_iota(jnp.int32, (T, T), 0)
    col = lax.broadcasted_iota(jnp.int32, (T, T), 1)
    valid = (col >= row) if rev else (col <= row)
    valid_t = (row >= col) if rev else (row <= col)
    b_c = jnp.sum(jnp.where(valid, lf_r, 0.0), axis=1, keepdims=True)
    b_r = jnp.sum(jnp.where(valid_t, lf_c, 0.0), axis=0, keepdims=True)
    m = m_s[...]
    d = jnp.where(valid, b_c - b_r + li_r, -jnp.inf)
    inter = b_c + m
    m_t = jnp.maximum(inter, jnp.max(d, axis=1, keepdims=True))
    q, k, v = q_ref[...], k_ref[...], v_ref[...]
    qk = lax.dot_general(q, k, (((1,), (1,)), ((), ())), preferred_element_type=F32)
    s = qk * scale * jnp.exp(d - m_t)
    w_inter = jnp.exp(inter - m_t)
    qc = jnp.dot(q, c_s[...].astype(BF16), preferred_element_type=F32) * scale
    num = jnp.dot(s.astype(BF16), v, preferred_element_type=F32) + w_inter * qc
    qn = jnp.sum(q.astype(F32) * n_s[...], axis=1, keepdims=True) * scale
    den = jnp.sum(s, axis=1, keepdims=True) + w_inter * qn
    hout = num / jnp.maximum(jnp.abs(den), jnp.exp(-m_t))

    b_last = b_c[0:1, :] if rev else b_c[T - 1:T, :]
    w_c = b_last - b_c + li_c
    m_new = jnp.maximum(b_last + m, jnp.max(w_c, axis=0, keepdims=True))
    kw = k.astype(F32) * jnp.exp(w_c - m_new)
    decay = jnp.exp(b_last + m - m_new)
    c_s[...] = decay * c_s[...] + lax.dot_general(
        kw.astype(BF16), v, (((0,), (0,)), ((), ())), preferred_element_type=F32)
    n_s[...] = decay * n_s[...] + jnp.sum(kw, axis=0, keepdims=True)
    m_s[...] = m_new

    if final:
        hsum = hout + hprev_ref[...]
        mu = jnp.mean(hsum, axis=1, keepdims=True)
        var = jnp.mean(jnp.square(hsum - mu), axis=1, keepdims=True)
        hn = (hsum - mu) * lax.rsqrt(var + LN_EPS) * ng_ref[...]
        out_ref[...] = (jax.nn.sigmoid(o_ref[...].astype(F32)) * hn).astype(out_ref.dtype)
    else:
        out_ref[...] = hout


def _mlstm(zq, gates_c, gates_r, bias_c, bias_r, B, L, dk, dv, rev, hprev=None, norm_g=None):
    H = ML_HEADS
    T = min(ML_CHUNK, L)
    nc = L // T
    final = hprev is not None
    cidx = (lambda c: nc - 1 - c) if rev else (lambda c: c)
    kq, kv = H * dk // dk, (2 * H * dk) // dv
    in_specs = [
        pl.BlockSpec((T, dk), lambda b, h, c: (b * nc + cidx(c), h)),
        pl.BlockSpec((T, dk), lambda b, h, c: (b * nc + cidx(c), kq + h)),
        pl.BlockSpec((T, dv), lambda b, h, c: (b * nc + cidx(c), kv + h)),
        pl.BlockSpec((None, None, T, 4), lambda b, h, c: (b, h, cidx(c), 0)),
        pl.BlockSpec((None, None, 4, T), lambda b, h, c: (b, h, 0, cidx(c))),
        pl.BlockSpec((None, 1, 4), lambda b, h, c: (h, 0, 0)),
        pl.BlockSpec((None, 4, 1), lambda b, h, c: (h, 0, 0)),
    ]
    args = [zq, zq, zq, gates_c, gates_r, bias_c, bias_r]
    if final:
        in_specs += [
            pl.BlockSpec((T, dv), lambda b, h, c: (b * nc + cidx(c), h)),
            pl.BlockSpec((T, dv), lambda b, h, c: (b * nc + cidx(c), kv + H + h)),
            pl.BlockSpec((1, dv), lambda b, h, c: (0, h)),
        ]
        args += [hprev, zq, norm_g]
    return pl.pallas_call(
        functools.partial(_mlstm_body, rev=rev, T=T, scale=dk ** -0.5, final=final),
        grid=(B, H, nc),
        in_specs=in_specs,
        out_specs=pl.BlockSpec((T, dv), lambda b, h, c: (b * nc + cidx(c), h)),
        out_shape=jax.ShapeDtypeStruct((B * L, H * dv), BF16 if final else F32),
        scratch_shapes=[pltpu.VMEM((dk, dv), F32), pltpu.VMEM((1, dk), F32), pltpu.VMEM((1, 1), F32)],
        compiler_params=_cparams("parallel", "parallel", "arbitrary"),
        name="mlstm_bwd" if rev else "mlstm_fwd",
    )(*args)


def _merge_body(yh_ref, ym_ref, ph_ref, pm_ref, gh_ref, gm_ref, o_ref):
    a = jnp.dot(yh_ref[...], ph_ref[...], preferred_element_type=F32)
    b = jnp.dot(ym_ref[...], pm_ref[...], preferred_element_type=F32)
    o_ref[...] = (jax.nn.sigmoid(gh_ref[...].astype(F32)) * a
                  + jax.nn.sigmoid(gm_ref[...].astype(F32)) * b).astype(o_ref.dtype)


def _merge(y_hy, y_ml, p_hy, p_ml, gates, tm=512, tn=512):
    n, kh = y_hy.shape
    km = y_ml.shape[1]
    d = p_hy.shape[1]
    nj = d // tn
    return pl.pallas_call(
        _merge_body,
        grid=(n // tm, nj),
        in_specs=[pl.BlockSpec((tm, kh), lambda i, j: (i, 0)),
                  pl.BlockSpec((tm, km), lambda i, j: (i, 0)),
                  pl.BlockSpec((kh, tn), lambda i, j: (0, j)),
                  pl.BlockSpec((km, tn), lambda i, j: (0, j)),
                  pl.BlockSpec((tm, tn), lambda i, j: (i, j)),
                  pl.BlockSpec((tm, tn), lambda i, j: (i, nj + j))],
        out_specs=pl.BlockSpec((tm, tn), lambda i, j: (i, j)),
        out_shape=jax.ShapeDtypeStruct((n, d), BF16),
        compiler_params=_cparams("parallel", "arbitrary"),
        name="gated_merge",
    )(y_hy, y_ml, p_hy, p_ml, gates, gates)


def _layer_norm(x, g, b):
    mu = jnp.mean(x, axis=-1, keepdims=True)
    var = jnp.mean(jnp.square(x - mu), axis=-1, keepdims=True)
    return (x - mu) * lax.rsqrt(var + LN_EPS) * g + b


def _proj_ln_body(a_ref, w_ref, x_ref, g_ref, b_ref, o_ref, acc):
    k = pl.program_id(1)

    @pl.when(k == 0)
    def _():
        acc[...] = jnp.zeros_like(acc)

    acc[...] += jnp.dot(a_ref[...], w_ref[...], preferred_element_type=F32)

    @pl.when(k == pl.num_programs(1) - 1)
    def _():
        o_ref[...] = _layer_norm(DEEPNORM_ALPHA * x_ref[...] + acc[...], g_ref[...], b_ref[...])


def _proj_ln(a, w, x, g, b, tm=512, tk=512):
    n, kd = a.shape
    d = w.shape[1]
    return pl.pallas_call(
        _proj_ln_body,
        grid=(n // tm, kd // tk),
        in_specs=[pl.BlockSpec((tm, tk), lambda i, k: (i, k)),
                  pl.BlockSpec((tk, d), lambda i, k: (k, 0)),
                  pl.BlockSpec((tm, d), lambda i, k: (i, 0), pipeline_mode=pl.Buffered(1)),
                  pl.BlockSpec((1, d), lambda i, k: (0, 0)),
                  pl.BlockSpec((1, d), lambda i, k: (0, 0))],
        out_specs=pl.BlockSpec((tm, d), lambda i, k: (i, 0)),
        out_shape=jax.ShapeDtypeStruct((n, d), F32),
        scratch_shapes=[pltpu.VMEM((tm, d), F32)],
        compiler_params=_cparams("parallel", "arbitrary"),
        name="out_proj_ln",
    )(a, w, x, g.reshape(1, d), b.reshape(1, d))


def _router_body(x_ref, w_ref, b_ref, wout_ref, eout_ref, cnt_ref):
    G, PG = MOE_GROUPS, MOE_PER_GROUP

    @pl.when(pl.program_id(0) == 0)
    def _():
        cnt_ref[...] = jnp.zeros_like(cnt_ref)

    logits = jnp.dot(x_ref[...], w_ref[...], precision=HIGHEST, preferred_element_type=F32) + b_ref[...]
    lane = lax.broadcasted_iota(jnp.int32, logits.shape, 1)
    ninf = -jnp.inf
    first = lambda mask: jnp.min(jnp.where(mask, lane, 2 * LANES), axis=1, keepdims=True)
    lg1 = jnp.where(lane < G, logits, ninf)
    m1 = jnp.max(lg1, axis=1, keepdims=True)
    g_sel = first(lg1 == m1)
    p_group = 1.0 / jnp.sum(jnp.exp(lg1 - m1), axis=1, keepdims=True)
    lo = G + g_sel * PG
    in_grp = jnp.logical_and(lane >= lo, lane < lo + PG)
    lg2 = jnp.where(in_grp, logits, ninf)
    m2 = jnp.max(lg2, axis=1, keepdims=True)
    e2 = jnp.exp(lg2 - m2)
    p2 = jnp.where(in_grp, e2 / jnp.sum(e2, axis=1, keepdims=True), -1.0)
    t1 = jnp.max(p2, axis=1, keepdims=True)
    j1 = first(p2 == t1)
    p2b = jnp.where(lane == j1, -1.0, p2)
    t2 = jnp.max(p2b, axis=1, keepdims=True)
    j2 = first(p2b == t2)
    tot = t1 + t2
    oh1, oh2 = lane == j1, lane == j2
    ohs = jnp.where(jnp.logical_or(oh1, oh2), 1.0, 0.0)
    tm = ohs.shape[0]
    earlier = (lax.broadcasted_iota(jnp.int32, (tm, tm), 1)
               < lax.broadcasted_iota(jnp.int32, (tm, tm), 0))
    base = cnt_ref[...] + jnp.dot(jnp.where(earlier, 1.0, 0.0).astype(BF16), ohs.astype(BF16),
                                  preferred_element_type=F32)
    r1 = jnp.sum(jnp.where(oh1, base, 0.0), axis=1, keepdims=True).astype(jnp.int32)
    r2 = jnp.sum(jnp.where(oh2, base, 0.0), axis=1, keepdims=True).astype(jnp.int32)
    cnt_ref[...] += jnp.sum(ohs, axis=0, keepdims=True)
    wout_ref[...] = jnp.where(lane == 0, p_group * (t1 / tot),
                              jnp.where(lane == 1, p_group * (t2 / tot), 0.0))
    eout_ref[...] = jnp.where(lane == 0, j1 - G, jnp.where(lane == 1, j2 - G,
                              jnp.where(lane == 2, r1, jnp.where(lane == 3, r2, 0))))


def _router(x, router_w1, router_b1, router_w2, router_b2, tm=256):
    n, d = x.shape
    ncol = MOE_GROUPS + MOE_GROUPS * MOE_PER_GROUP
    w = jnp.zeros((d, LANES), F32).at[:, :ncol].set(jnp.concatenate([router_w1, router_w2], axis=1))
    b = jnp.zeros((1, LANES), F32).at[0, :ncol].set(jnp.concatenate([router_b1, router_b2]))
    return pl.pallas_call(
        _router_body,
        grid=(n // tm,),
        in_specs=[pl.BlockSpec((tm, d), lambda i: (i, 0)),
                  pl.BlockSpec((d, LANES), lambda i: (0, 0)),
                  pl.BlockSpec((1, LANES), lambda i: (0, 0))],
        out_specs=[pl.BlockSpec((tm, LANES), lambda i: (i, 0)),
                   pl.BlockSpec((tm, LANES), lambda i: (i, 0)),
                   pl.BlockSpec((1, LANES), lambda i: (0, 0))],
        out_shape=[jax.ShapeDtypeStruct((n, LANES), F32), jax.ShapeDtypeStruct((n, LANES), jnp.int32),
                   jax.ShapeDtypeStruct((1, LANES), F32)],
        compiler_params=_cparams("arbitrary"),
        name="moe_router",
    )(x, w, b)


def _row_copy(src_hbm, row, dst_vmem, r, sem):
    return pltpu.make_async_copy(src_hbm.at[pl.ds(row, 1), :], dst_vmem.at[pl.ds(r, 1), :], sem)


def _expert_body(tok_ref, blk_e_ref, nused_ref, x_hbm, w1_ref, w3_ref, w2_ref, o_ref, xbuf, sem, *, bm):
    del blk_e_ref
    i = pl.program_id(0)
    n_used = nused_ref[0]

    def gather(blk, slot):
        def issue(r, c):
            _row_copy(x_hbm, tok_ref[blk * bm + r], xbuf.at[slot], r, sem.at[slot]).start()
            return c
        lax.fori_loop(0, bm, issue, 0)

    @pl.when(i == 0)
    def _():
        gather(0, 0)

    @pl.when(i + 1 < n_used)
    def _():
        gather(i + 1, (i + 1) % 2)

    @pl.when(i < n_used)
    def _():
        slot = i % 2

        def wait(r, c):
            _row_copy(x_hbm, 0, xbuf.at[slot], r, sem.at[slot]).wait()
            return c
        lax.fori_loop(0, bm, wait, 0)
        xb = xbuf[slot].astype(BF16)
        a = jnp.dot(xb, w1_ref[...], preferred_element_type=F32)
        b = jnp.dot(xb, w3_ref[...], preferred_element_type=F32)
        hb = (a * jax.nn.sigmoid(a)) * b
        o_ref[...] = jnp.dot(hb.astype(BF16), w2_ref[...], preferred_element_type=F32)

    @pl.when(i >= n_used)
    def _():
        o_ref[...] = jnp.zeros_like(o_ref)


def _experts(x, slot_tok, blk_e, n_used, w1, w3, w2, bm):
    n, d = x.shape
    e, _, hd = w1.shape
    n_blocks = blk_e.shape[0]
    grid_spec = pltpu.PrefetchScalarGridSpec(
        num_scalar_prefetch=3,
        grid=(n_blocks,),
        in_specs=[pl.BlockSpec(memory_space=pl.ANY),
                  pl.BlockSpec((None, d, hd), lambda i, tok, be, nu: (be[i], 0, 0)),
                  pl.BlockSpec((None, d, hd), lambda i, tok, be, nu: (be[i], 0, 0)),
                  pl.BlockSpec((None, hd, d), lambda i, tok, be, nu: (be[i], 0, 0))],
        out_specs=pl.BlockSpec((bm, d), lambda i, tok, be, nu: (i, 0)),
        scratch_shapes=[pltpu.VMEM((2, bm, d), F32), pltpu.SemaphoreType.DMA((2,))],
    )
    return pl.pallas_call(
        functools.partial(_expert_body, bm=bm),
        grid_spec=grid_spec,
        out_shape=jax.ShapeDtypeStruct((n_blocks * bm, d), F32),
        compiler_params=_cparams("arbitrary"),
        name="moe_experts",
    )(slot_tok, blk_e, n_used, x, w1, w3, w2)


def _combine_body(slot_ref, y_hbm, w_ref, x_ref, g_ref, b_ref, o_ref, buf, sem, *, tb):
    i = pl.program_id(0)

    def issue(r, c):
        for kk in range(MOE_TOPK):
            _row_copy(y_hbm, slot_ref[(i * tb + r) * MOE_TOPK + kk], buf.at[kk], r, sem).start()
        return c
    lax.fori_loop(0, tb, issue, 0)

    def wait(r, c):
        for kk in range(MOE_TOPK):
            _row_copy(y_hbm, 0, buf.at[kk], r, sem).wait()
        return c
    lax.fori_loop(0, tb, wait, 0)
    w = w_ref[...]
    y = w[:, 0:1] * buf[0] + w[:, 1:2] * buf[1]
    o_ref[...] = _layer_norm(DEEPNORM_ALPHA * x_ref[...] + y, g_ref[...], b_ref[...])


def _combine_ln(slot_of, yb, weights, x, g, b, tb):
    n, d = x.shape
    grid_spec = pltpu.PrefetchScalarGridSpec(
        num_scalar_prefetch=1,
        grid=(n // tb,),
        in_specs=[pl.BlockSpec(memory_space=pl.ANY),
                  pl.BlockSpec((tb, LANES), lambda i, s: (i, 0)),
                  pl.BlockSpec((tb, d), lambda i, s: (i, 0)),
                  pl.BlockSpec((1, d), lambda i, s: (0, 0)),
                  pl.BlockSpec((1, d), lambda i, s: (0, 0))],
        out_specs=pl.BlockSpec((tb, d), lambda i, s: (i, 0)),
        scratch_shapes=[pltpu.VMEM((MOE_TOPK, tb, d), F32), pltpu.SemaphoreType.DMA(())],
    )
    return pl.pallas_call(
        functools.partial(_combine_body, tb=tb),
        grid_spec=grid_spec,
        out_shape=jax.ShapeDtypeStruct((n, d), F32),
        compiler_params=_cparams("arbitrary"),
        name="moe_combine_ln",
    )(slot_of, yb, weights, x, g.reshape(1, d), b.reshape(1, d))


def _moe(h1, router_w1, router_b1, router_w2, router_b2, exp_w1, exp_w3, exp_w2, ln_g, ln_b):
    n, d = h1.shape
    e = exp_w1.shape[0]
    bm = MOE_BM
    weights, ids, cnt = _router(h1, router_w1, router_b1, router_w2, router_b2)
    m = n * MOE_TOPK
    eid_f = ids[:, :MOE_TOPK].reshape(m)
    rank = ids[:, MOE_TOPK:2 * MOE_TOPK].reshape(m)
    counts = cnt[0, MOE_GROUPS:MOE_GROUPS + e].astype(jnp.int32)
    nblk_e = (counts + bm - 1) // bm
    bend = jnp.cumsum(nblk_e)
    pstart = (bend - nblk_e) * bm
    slot_of = (pstart[eid_f] + rank).astype(jnp.int32)
    n_blocks = -(-m // bm) + e
    tok_f = jnp.arange(m, dtype=jnp.int32) // MOE_TOPK
    slot_tok = jnp.zeros((n_blocks * bm,), jnp.int32).at[slot_of].set(tok_f)
    n_used = bend[-1:].astype(jnp.int32)
    blk = jnp.minimum(jnp.arange(n_blocks, dtype=jnp.int32), n_used[0] - 1)
    blk_e = jnp.minimum(jnp.sum(bend[None, :] <= blk[:, None], axis=1), e - 1).astype(jnp.int32)
    yb = _experts(h1, slot_tok, blk_e, n_used, exp_w1.astype(BF16), exp_w3.astype(BF16),
                  exp_w2.astype(BF16), bm)
    return _combine_ln(slot_of, yb, weights, h1, ln_g, ln_b, min(MOE_TB, n))


def kernel(x, w_in, hy_conv_w, hy_conv_b, hy_f_w1, hy_f_b1, hy_f_fr1, hy_f_w2, hy_f_b2, hy_f_fr2,
           hy_f_w3, hy_f_b3, hy_f_fr3, hy_f_wout, hy_bias, ml_gate_bias, ml_norm_g, p_hy, p_ml, w_out,
           ln1_g, ln1_b, router_w1, router_b1, router_w2, router_b2, exp_w1, exp_w3, exp_w2,
           ln2_g, ln2_b):
    B, L, D = x.shape
    N = B * L
    C = D // 2
    H = ML_HEADS
    dv = C // H
    dk = dv // 2
    col_q = (HY_ORDER + 1) * C
    col_if = col_q + 2 * H * dk + 2 * C
    col_gate = col_if + 4 * H

    xf = x.reshape(N, D)
    xb = xf.astype(BF16)
    w_hy = w_in[:, :col_q].astype(BF16)
    w_ml = w_in[:, col_q:col_if].astype(BF16)
    w_if = jnp.zeros((D, LANES), BF16).at[:, :4 * H].set(w_in[:, col_if:col_gate].astype(BF16))
    w_gt = w_in[:, col_gate:].astype(BF16)

    z_hy = _matmul(xb, w_hy, F32, 1024, 512).reshape(B, L, col_q)
    z_ml = _matmul(xb, w_ml, BF16, 1024, 512)
    z_if = _matmul(xb, w_if, F32, 1024, LANES)
    z_gt = _matmul(xb, w_gt, BF16, 1024, 512)

    tables, filter_tables = _dft_tables(L)
    kern = _hyena_filters(L, C, hy_f_w1, hy_f_b1, hy_f_fr1, hy_f_w2, hy_f_b2, hy_f_fr2,
                          hy_f_w3, hy_f_b3, hy_f_fr3, hy_f_wout, hy_bias)
    n2 = 2 * L // FFT_N1
    kfreq = _filter_fft(_row_permute(kern, FFT_N1, n2, F32), *filter_tables)
    u = _short_conv(z_hy, hy_conv_w, hy_conv_b, n2)
    v1 = _long_conv(u, 2, u, 0, kfreq, 0, tables, C)
    y_hy = _long_conv(v1, 0, u, 1, kfreq, 1, tables, C)
    y_hy = _row_permute(y_hy, n2, L // n2, BF16).reshape(N, C)

    g = z_if[:, :4 * H].reshape(B, L, 4, H)
    gates_c = g.transpose(0, 3, 1, 2)
    gates_r = g.transpose(0, 3, 2, 1)
    bias_c = ml_gate_bias.T.reshape(H, 1, 4)
    bias_r = ml_gate_bias.T.reshape(H, 4, 1)
    h_fwd = _mlstm(z_ml, gates_c, gates_r, bias_c, bias_r, B, L, dk, dv, rev=False)
    y_ml = _mlstm(z_ml, gates_c, gates_r, bias_c, bias_r, B, L, dk, dv, rev=True,
                  hprev=h_fwd, norm_g=ml_norm_g.reshape(1, C))

    merged = _merge(y_hy, y_ml, p_hy.astype(BF16), p_ml.astype(BF16), z_gt)
    h1 = _proj_ln(merged, w_out.astype(BF16), xf, ln1_g, ln1_b)
    out = _moe(h1, router_w1, router_b1, router_w2, router_b2, exp_w1, exp_w3, exp_w2, ln2_g, ln2_b)
    return out.reshape(B, L, D)
```

```python
import functools
import math

import jax
import jax.numpy as jnp
from jax import lax
from jax.experimental import pallas as pl
from jax.experimental.pallas import tpu as pltpu

F32 = jnp.float32
BF16 = jnp.bfloat16
HIGHEST = lax.Precision.HIGHEST

VMEM_LIMIT_BYTES = 56 * 1024 * 1024
LANES = 128

HY_ORDER = 2
HY_SHORT = 3
HY_POS_EMB = 33
HY_DECAY_TARGET = 1e-2
HY_FAST_DECAY = 0.3
HY_SLOW_DECAY = 1.5
HY_MOD_SHIFT = 0.05
ML_HEADS = 8
MOE_GROUPS = 8
MOE_PER_GROUP = 8
MOE_TOPK = 2
DEPTH = 1
DEEPNORM_ALPHA = (2.0 * DEPTH) ** 0.25
LN_EPS = 1e-5

FFT_N1 = 64
HY_CB = 256
FFT_G = 32
FFT_KC = 16
FFT_UNROLL_OUTER = 4
FFT_UNROLL_INNER = 4
ML_CHUNK = 256
MOE_BM = 128
MOE_RB = 5
MOE_HC = 256
MOE_OC = 1024
MOE_TB = 256


def _cparams(*sem):
    return pltpu.CompilerParams(dimension_semantics=sem, vmem_limit_bytes=VMEM_LIMIT_BYTES)


def _mm_body(a_ref, b_ref, o_ref):
    o_ref[...] = jnp.dot(a_ref[...], b_ref[...], preferred_element_type=F32).astype(o_ref.dtype)


def _matmul(a, b, out_dtype, tm, tn):
    m, k = a.shape
    _, n = b.shape
    assert m % tm == 0 and n % tn == 0
    return pl.pallas_call(
        _mm_body,
        grid=(m // tm, n // tn),
        in_specs=[pl.BlockSpec((tm, k), lambda i, j: (i, 0)),
                  pl.BlockSpec((k, tn), lambda i, j: (0, j))],
        out_specs=pl.BlockSpec((tm, tn), lambda i, j: (i, j)),
        out_shape=jax.ShapeDtypeStruct((m, n), out_dtype),
        compiler_params=_cparams("parallel", "arbitrary"),
        name="proj_matmul",
    )(a, b)


def _permute_pitch(outer):
    return outer + 8


def _permute_rows(get_rows, tmp_ref, dst_ref, inner, outer):
    pitch = _permute_pitch(outer)
    for b in range(inner):
        tmp_ref[0, b * pitch:b * pitch + outer, :] = get_rows(b * outer, (b + 1) * outer)

    def it(a, c):
        rows = tmp_ref[0, pl.ds(a, inner, stride=pitch), :]
        dst_ref[0, pl.ds(pl.multiple_of(a * inner, inner), inner), :] = rows.astype(dst_ref.dtype)
        return c
    lax.fori_loop(0, outer, it, 0, unroll=4)


def _short_conv_body(z_ref, w_ref, b_ref, o_ref, tmp, *, n2):
    z = z_ref[0]
    L = z.shape[0]
    row = lax.broadcasted_iota(jnp.int32, z.shape, 0)
    prev = jnp.where(row == 0, 0.0, pltpu.roll(z, 1, 0))
    nxt = jnp.where(row == L - 1, 0.0, pltpu.roll(z, L - 1, 0))
    u = b_ref[...] + prev * w_ref[0:1, :] + z * w_ref[1:2, :] + nxt * w_ref[2:3, :]
    _permute_rows(lambda lo, hi: u[lo:hi], tmp, o_ref, L // n2, n2)


def _short_conv(z, w, b, n2, cb=LANES):
    B, L, C = z.shape
    return pl.pallas_call(
        functools.partial(_short_conv_body, n2=n2),
        grid=(B, C // cb),
        in_specs=[pl.BlockSpec((1, L, cb), lambda i, j: (i, 0, j)),
                  pl.BlockSpec((HY_SHORT, cb), lambda i, j: (0, j)),
                  pl.BlockSpec((1, cb), lambda i, j: (0, j))],
        out_specs=pl.BlockSpec((1, L, cb), lambda i, j: (i, 0, j)),
        out_shape=jax.ShapeDtypeStruct((B, L, C), F32),
        scratch_shapes=[pltpu.VMEM((1, (L // n2) * _permute_pitch(n2), cb), F32)],
        compiler_params=_cparams("parallel", "parallel"),
        name="hy_short_conv",
    )(z, w, b.reshape(1, C))


def _row_permute_body(x_ref, o_ref, tmp, *, inner, outer):
    _permute_rows(lambda lo, hi: x_ref[0, lo:hi, :], tmp, o_ref, inner, outer)


def _row_permute(x, inner, outer, out_dtype, cb=LANES):
    A, R, C = x.shape
    assert R == inner * outer
    cb = min(cb, C)
    return pl.pallas_call(
        functools.partial(_row_permute_body, inner=inner, outer=outer),
        grid=(A, C // cb),
        in_specs=[pl.BlockSpec((1, R, cb), lambda i, j: (i, 0, j))],
        out_specs=pl.BlockSpec((1, R, cb), lambda i, j: (i, 0, j)),
        out_shape=jax.ShapeDtypeStruct((A, R, C), out_dtype),
        scratch_shapes=[pltpu.VMEM((1, inner * _permute_pitch(outer), cb), F32)],
        compiler_params=_cparams("parallel", "parallel"),
        name="hy_row_permute",
    )(x)


def _filter_hidden_body(w1_ref, b1_ref, fr1_ref, w2_ref, b2_ref, fr2_ref, w3_ref, b3_ref, fr3_ref,
                        o_ref, *, L, rows):
    i = pl.program_id(0)
    n = i * rows + lax.broadcasted_iota(jnp.int32, (rows, 1), 0)
    pos = jnp.where(n < L, n, 2 * L - n).astype(F32)
    t = pos / (L - 1.0)
    w = (2.0 * math.pi / L) * pos
    lane = lax.broadcasted_iota(jnp.int32, (1, LANES), 1)
    bands = (HY_POS_EMB - 1) // 2
    band = jnp.where(lane <= bands, lane - 1, lane - 1 - bands).astype(F32)
    freq = 1e-4 + band * ((bands - 1 - 1e-4) / (bands - 1))
    ang = w * freq
    feats = jnp.where(lane == 0, t,
                      jnp.where(lane <= bands, jnp.cos(ang),
                                jnp.where(lane <= 2 * bands, -jnp.sin(ang), 0.0)))
    h = jnp.sin(fr1_ref[...] * (jnp.dot(feats, w1_ref[...], precision=HIGHEST,
                                        preferred_element_type=F32) + b1_ref[...]))
    h = jnp.sin(fr2_ref[...] * (jnp.dot(h, w2_ref[...], precision=HIGHEST,
                                        preferred_element_type=F32) + b2_ref[...]))
    h = jnp.sin(fr3_ref[...] * (jnp.dot(h, w3_ref[...], precision=HIGHEST,
                                        preferred_element_type=F32) + b3_ref[...]))
    o_ref[...] = h


def _filter_out_body(h_ref, wout_ref, delta_ref, bias_ref, o_ref, *, L):
    d = pl.program_id(1)
    h = jnp.dot(h_ref[...], wout_ref[0], precision=HIGHEST, preferred_element_type=F32)
    r = lax.broadcasted_iota(jnp.int32, (L, 1), 0)
    pos = jnp.where(d == 0, r, L - r).astype(F32)
    t = pos / (L - 1.0)
    window = jnp.exp(-t * delta_ref[...]) + HY_MOD_SHIFT
    first = r == 0
    tap = jnp.where(jnp.logical_and(d == 1, first), 0.0, h * window)
    o_ref[0] = tap + jnp.where(jnp.logical_and(d == 0, first), bias_ref[...], 0.0)


def _hyena_filters(L, C, f_w1, f_b1, f_fr1, f_w2, f_b2, f_fr2, f_w3, f_b3, f_fr3, f_wout, bias, cb=512):
    fh = f_w2.shape[0]
    rows = 1024 if (2 * L) % 1024 == 0 else 2 * L
    w1p = jnp.zeros((LANES, fh), F32).at[:HY_POS_EMB].set(f_w1)
    vec = lambda a: a.reshape(1, fh)
    full = lambda shape: pl.BlockSpec(shape, lambda i: (0,) * len(shape))
    hid = pl.pallas_call(
        functools.partial(_filter_hidden_body, L=L, rows=rows),
        grid=(2 * L // rows,),
        in_specs=[full((LANES, fh)), full((1, fh)), full((1, fh)),
                  full((fh, fh)), full((1, fh)), full((1, fh)),
                  full((fh, fh)), full((1, fh)), full((1, fh))],
        out_specs=pl.BlockSpec((rows, fh), lambda i: (i, 0)),
        out_shape=jax.ShapeDtypeStruct((2 * L, fh), F32),
        compiler_params=_cparams("parallel"),
        name="hy_filter_hidden",
    )(w1p, vec(f_b1), vec(f_fr1), f_w2, vec(f_b2), vec(f_fr2), f_w3, vec(f_b3), vec(f_fr3))
    wout = f_wout.reshape(fh, 2, HY_ORDER, C).transpose(1, 2, 0, 3).reshape(2 * HY_ORDER, fh, C)
    deltas = jnp.abs(jnp.linspace(math.log(HY_DECAY_TARGET) / HY_SLOW_DECAY,
                                  math.log(HY_DECAY_TARGET) / HY_FAST_DECAY, C, dtype=F32))
    cb = min(cb, C)
    return pl.pallas_call(
        functools.partial(_filter_out_body, L=L),
        grid=(HY_ORDER, 2, C // cb),
        in_specs=[pl.BlockSpec((L, fh), lambda o, d, j: (d, 0)),
                  pl.BlockSpec((1, fh, cb), lambda o, d, j: (d * HY_ORDER + o, 0, j)),
                  pl.BlockSpec((1, cb), lambda o, d, j: (0, j)),
                  pl.BlockSpec((None, 1, cb), lambda o, d, j: (o, 0, j))],
        out_specs=pl.BlockSpec((1, L, cb), lambda o, d, j: (o, d, j)),
        out_shape=jax.ShapeDtypeStruct((HY_ORDER, 2 * L, C), F32),
        compiler_params=_cparams("parallel", "parallel", "parallel"),
        name="hy_filter_out",
    )(hid, wout, deltas.reshape(1, C), bias.reshape(HY_ORDER, 1, C))


def _dft_tables(L):
    N = 2 * L
    N1 = FFT_N1
    N2 = N // N1
    h = N1 // 2
    n2 = jnp.arange(N2, dtype=jnp.int32)[:, None, None]
    k1 = jnp.arange(N1, dtype=jnp.int32)[None, :, None]
    n1 = jnp.arange(N1, dtype=jnp.int32)[None, None, :]
    ph = (k1 * (N2 * n1 + n2)) % N
    ang = ph.astype(F32) * (-2.0 * math.pi / N)
    mr, mi = jnp.cos(ang), jnp.sin(ang)
    mrp, mip = mr[:, :, :h], mi[:, :, :h]
    t1 = jnp.concatenate([jnp.concatenate([mrp, -mip], axis=2),
                          jnp.concatenate([mip, mrp], axis=2)], axis=1)
    mrt, mit = jnp.swapaxes(mrp, 1, 2) / N, jnp.swapaxes(mip, 1, 2) / N
    t1i = jnp.concatenate([jnp.concatenate([mrt, mit], axis=2),
                           jnp.concatenate([-mit, mrt], axis=2)], axis=1)
    t1f = jnp.concatenate([mr, mi], axis=1)
    a = jnp.arange(N2, dtype=jnp.int32)
    ang2 = ((a[:, None] * a[None, :]) % N2).astype(F32) * (-2.0 * math.pi / N2)
    fr, fi = jnp.cos(ang2), jnp.sin(ang2)
    t2 = jnp.concatenate([jnp.concatenate([fr, -fi], axis=1),
                          jnp.concatenate([fi, fr], axis=1)], axis=0)
    t2i = jnp.concatenate([jnp.concatenate([fr, fi], axis=1),
                           jnp.concatenate([-fi, fr], axis=1)], axis=0)
    conv_tables = tuple(t.astype(BF16) for t in (t1, t1i, t2, t2i))
    filter_tables = (_hi_lo_rows(t1f), _hi_lo_rows(t2))
    return conv_tables, filter_tables


def _split_bf16(d):
    hi = d.astype(BF16)
    return hi, (d - hi.astype(F32)).astype(BF16)


def _hi_lo_rows(t):
    hi, lo = _split_bf16(t)
    return jnp.concatenate([hi, lo], axis=-2)


def _ld_lanes(ref, rows):
    return jnp.concatenate([ref[i, rows, :] for i in range(ref.shape[0])], axis=1)


def _st_lanes(ref, rows, val):
    for i in range(ref.shape[0]):
        ref[i, rows, :] = val[:, i * LANES:(i + 1) * LANES]


def _dot3(t, d, m):
    d_hi, d_lo = _split_bf16(d)
    y = jnp.dot(t, d_hi, preferred_element_type=F32)
    return y[:m] + y[m:] + jnp.dot(t[:m], d_lo, preferred_element_type=F32)


def _dot1(t, d):
    return jnp.dot(t, d.astype(BF16), preferred_element_type=F32)


def _work_pitch(N2):
    return N2 + 8


def _fft_steps(N1, N2):
    return min(FFT_G, N2), min(FFT_KC, N1)


def _filter_fft_body(k_ref, t1f_ref, t2_ref, o_ref, sr, si, *, N1, N2, G, KC):
    s = pl.program_id(2)
    sa = N2 // G
    P = _work_pitch(N2)

    @pl.when(s < sa)
    def _():
        def it(r, c):
            n2 = s * G + r
            rows = k_ref[pl.ds(pl.multiple_of(r * N1, N1), N1), :]
            a = _dot3(t1f_ref[n2], rows, 2 * N1)
            _st_lanes(sr, pl.ds(n2, N1, stride=P), a[:N1])
            _st_lanes(si, pl.ds(n2, N1, stride=P), a[N1:])
            return c
        lax.fori_loop(0, G, it, 0, unroll=FFT_UNROLL_OUTER)

    @pl.when(s >= sa)
    def _():
        t2 = t2_ref[...]

        def it(kk, c):
            r0 = pl.multiple_of(((s - sa) * KC + kk) * P, 8)
            q0 = pl.multiple_of(kk * N2, N2)
            x = _dot3(t2, jnp.concatenate([_ld_lanes(sr, pl.ds(r0, N2)), _ld_lanes(si, pl.ds(r0, N2))],
                                          axis=0), 2 * N2)
            o_ref[0, pl.ds(q0, N2), :] = x[:N2]
            o_ref[1, pl.ds(q0, N2), :] = x[N2:]
            return c
        lax.fori_loop(0, KC, it, 0, unroll=FFT_UNROLL_INNER)


def _filter_fft(kern, t1f, t2):
    O, N, C = kern.shape
    N1, N2 = FFT_N1, N // FFT_N1
    G, KC = _fft_steps(N1, N2)
    sa, sb = N2 // G, N1 // KC
    cb = min(HY_CB, C)
    full = lambda shape: pl.BlockSpec(shape, lambda o, j, s: (0,) * len(shape),
                                      pipeline_mode=pl.Buffered(1))
    return pl.pallas_call(
        functools.partial(_filter_fft_body, N1=N1, N2=N2, G=G, KC=KC),
        grid=(O, C // cb, sa + sb),
        in_specs=[pl.BlockSpec((None, G * N1, cb), lambda o, j, s: (o, jnp.minimum(s, sa - 1), j)),
                  full(t1f.shape), full(t2.shape)],
        out_specs=pl.BlockSpec((None, 2, KC * N2, cb),
                               lambda o, j, s: (o, 0, jnp.maximum(s - sa, 0), j)),
        out_shape=jax.ShapeDtypeStruct((O, 2, N, C), F32),
        scratch_shapes=[pltpu.VMEM((cb // LANES, N1 * _work_pitch(N2), LANES), F32)] * 2,
        compiler_params=_cparams("parallel", "parallel", "arbitrary"),
        name="hy_filter_fft",
    )(kern, t1f, t2)


def _long_conv_body(v_ref, g_ref, kf_ref, t1_ref, t1i_ref, t2_ref, t2i_ref, o_ref, sr, si,
                    *, N1, N2, G, KC):
    s = pl.program_id(2)
    sa, sb = N2 // G, N1 // KC
    h = N1 // 2
    P = _work_pitch(N2)

    @pl.when(s < sa)
    def _():
        def it(r, c):
            n2 = s * G + r
            q = pl.ds(pl.multiple_of(r * h, h), h)
            d = jnp.concatenate([v_ref[0, q, :], v_ref[1, q, :]], axis=0)
            a = _dot1(t1_ref[n2], d)
            _st_lanes(sr, pl.ds(n2, N1, stride=P), a[:N1])
            _st_lanes(si, pl.ds(n2, N1, stride=P), a[N1:])
            return c
        lax.fori_loop(0, G, it, 0, unroll=FFT_UNROLL_OUTER)

    @pl.when(jnp.logical_and(s >= sa, s < sa + sb))
    def _():
        t2, t2i = t2_ref[...], t2i_ref[...]

        def it(kk, c):
            r0 = pl.multiple_of(((s - sa) * KC + kk) * P, 8)
            q0 = pl.multiple_of(kk * N2, N2)
            x = _dot1(t2, jnp.concatenate([_ld_lanes(sr, pl.ds(r0, N2)), _ld_lanes(si, pl.ds(r0, N2))],
                                          axis=0))
            xr, xi = x[:N2], x[N2:]
            kr = kf_ref[0, pl.ds(q0, N2), :]
            ki = kf_ref[1, pl.ds(q0, N2), :]
            p = jnp.concatenate([xr * kr - xi * ki, xr * ki + xi * kr], axis=0)
            y = _dot1(t2i, p)
            _st_lanes(sr, pl.ds(r0, N2), y[:N2])
            _st_lanes(si, pl.ds(r0, N2), y[N2:])
            return c
        lax.fori_loop(0, KC, it, 0, unroll=FFT_UNROLL_INNER)

    @pl.when(s >= sa + sb)
    def _():
        def it(r, c):
            n2 = (s - sa - sb) * G + r
            q = pl.ds(pl.multiple_of(r * h, h), h)
            d = jnp.concatenate([_ld_lanes(sr, pl.ds(n2, N1, stride=P)),
                                 _ld_lanes(si, pl.ds(n2, N1, stride=P))], axis=0)
            y = _dot1(t1i_ref[n2], d)
            for b in range(2):
                o_ref[b, q, :] = g_ref[b, q, :] * y[b * h:(b + 1) * h]
            return c
        lax.fori_loop(0, G, it, 0, unroll=FFT_UNROLL_OUTER)


def _long_conv(v_arr, v_off, g_arr, g_off, kfreq, order, tables, C):
    B, L, _ = v_arr.shape
    t1, t1i, t2, t2i = tables
    N = 2 * L
    N1, N2 = FFT_N1, N // FFT_N1
    G, KC = _fft_steps(N1, N2)
    sa, sb = N2 // G, N1 // KC
    h = N1 // 2
    cb = min(HY_CB, C)
    nj = C // cb
    full = lambda shape: pl.BlockSpec(shape, lambda j, p, s: (0,) * len(shape),
                                      pipeline_mode=pl.Buffered(1))
    last = lambda s: jnp.clip(s - sa - sb, 0, sa - 1)
    return pl.pallas_call(
        functools.partial(_long_conv_body, N1=N1, N2=N2, G=G, KC=KC),
        grid=(nj, B // 2, 2 * sa + sb),
        in_specs=[pl.BlockSpec((2, G * h, cb), lambda j, p, s: (p, jnp.minimum(s, sa - 1), v_off * nj + j)),
                  pl.BlockSpec((2, G * h, cb), lambda j, p, s: (p, last(s), g_off * nj + j)),
                  pl.BlockSpec((None, 2, KC * N2, cb),
                               lambda j, p, s: (order, 0, jnp.clip(s - sa, 0, sb - 1), j)),
                  full(t1.shape), full(t1i.shape), full(t2.shape), full(t2i.shape)],
        out_specs=pl.BlockSpec((2, G * h, cb), lambda j, p, s: (p, last(s), j)),
        out_shape=jax.ShapeDtypeStruct((B, L, C), F32),
        scratch_shapes=[pltpu.VMEM((cb // LANES, N1 * _work_pitch(N2), LANES), F32)] * 2,
        compiler_params=_cparams("parallel", "arbitrary", "arbitrary"),
        name="hy_long_conv",
    )(v_arr, g_arr, kfreq, t1, t1i, t2, t2i)


def _log_sigmoid(x):
    return jnp.minimum(x, 0.0) - jnp.log1p(jnp.exp(-jnp.abs(x)))


def _mlstm_body(*refs, rev, T, scale, final):
    if final:
        (q_ref, k_ref, v_ref, gc_ref, gr_ref, bc_ref, br_ref, hprev_ref, o_ref, ng_ref,
         out_ref, c_s, n_s, m_s) = refs
    else:
        q_ref, k_ref, v_ref, gc_ref, gr_ref, bc_ref, br_ref, out_ref, c_s, n_s, m_s = refs

    @pl.when(pl.program_id(2) == 0)
    def _():
        c_s[...] = jnp.zeros_like(c_s)
        n_s[...] = jnp.zeros_like(n_s)
        m_s[...] = jnp.zeros_like(m_s)

    gi = 2 if rev else 0
    gc = gc_ref[...] + bc_ref[...]
    gr = gr_ref[...] + br_ref[...]
    li_c, lf_c = gc[:, gi:gi + 1], _log_sigmoid(gc[:, gi + 1:gi + 2])
    li_r, lf_r = gr[gi:gi + 1, :], _log_sigmoid(gr[gi + 1:gi + 2, :])
    row = lax.broadcasted_iota(jnp.int32, (T, T), 0)
    col = lax.broadcasted_iota(jnp.int32, (T, T), 1)
    valid = (col >= row) if rev else (col <= row)
    valid_t = (row >= col) if rev else (row <= col)
    b_c = jnp.sum(jnp.where(valid, lf_r, 0.0), axis=1, keepdims=True)
    b_r = jnp.sum(jnp.where(valid_t, lf_c, 0.0), axis=0, keepdims=True)
    m = m_s[...]
    d = jnp.where(valid, b_c - b_r + li_r, -jnp.inf)
    inter = b_c + m
    m_t = jnp.maximum(inter, jnp.max(d, axis=1, keepdims=True))
    q, k, v = q_ref[...], k_ref[...], v_ref[...]
    qk = lax.dot_general(q, k, (((1,), (1,)), ((), ())), preferred_element_type=F32)
    s = qk * scale * jnp.exp(d - m_t)
    w_inter = jnp.exp(inter - m_t)
    qc = jnp.dot(q, c_s[...].astype(BF16), preferred_element_type=F32) * scale
    num = jnp.dot(s.astype(BF16), v, preferred_element_type=F32) + w_inter * qc
    qn = jnp.sum(q.astype(F32) * n_s[...], axis=1, keepdims=True) * scale
    den = jnp.sum(s, axis=1, keepdims=True) + w_inter * qn
    hout = num / jnp.maximum(jnp.abs(den), jnp.exp(-m_t))

    b_last = b_c[0:1, :] if rev else b_c[T - 1:T, :]
    w_c = b_last - b_c + li_c
    m_new = jnp.maximum(b_last + m, jnp.max(w_c, axis=0, keepdims=True))
    kw = k.astype(F32) * jnp.exp(w_c - m_new)
    decay = jnp.exp(b_last + m - m_new)
    c_s[...] = decay * c_s[...] + lax.dot_general(
        kw.astype(BF16), v, (((0,), (0,)), ((), ())), preferred_element_type=F32)
    n_s[...] = decay * n_s[...] + jnp.sum(kw, axis=0, keepdims=True)
    m_s[...] = m_new

    if final:
        hsum = hout + hprev_ref[...]
        mu = jnp.mean(hsum, axis=1, keepdims=True)
        var = jnp.mean(jnp.square(hsum - mu), axis=1, keepdims=True)
        hn = (hsum - mu) * lax.rsqrt(var + LN_EPS) * ng_ref[...]
        out_ref[...] = (jax.nn.sigmoid(o_ref[...].astype(F32)) * hn).astype(out_ref.dtype)
    else:
        out_ref[...] = hout


def _mlstm(zq, gates_c, gates_r, bias_c, bias_r, B, L, dk, dv, rev, hprev=None, norm_g=None):
    H = ML_HEADS
    T = min(ML_CHUNK, L)
    nc = L // T
    final = hprev is not None
    cidx = (lambda c: nc - 1 - c) if rev else (lambda c: c)
    kq, kv = H * dk // dk, (2 * H * dk) // dv
    in_specs = [
        pl.BlockSpec((T, dk), lambda b, h, c: (b * nc + cidx(c), h)),
        pl.BlockSpec((T, dk), lambda b, h, c: (b * nc + cidx(c), kq + h)),
        pl.BlockSpec((T, dv), lambda b, h, c: (b * nc + cidx(c), kv + h)),
        pl.BlockSpec((None, None, T, 4), lambda b, h, c: (b, h, cidx(c), 0)),
        pl.BlockSpec((None, None, 4, T), lambda b, h, c: (b, h, 0, cidx(c))),
        pl.BlockSpec((None, 1, 4), lambda b, h, c: (h, 0, 0)),
        pl.BlockSpec((None, 4, 1), lambda b, h, c: (h, 0, 0)),
    ]
    args = [zq, zq, zq, gates_c, gates_r, bias_c, bias_r]
    if final:
        in_specs += [
            pl.BlockSpec((T, dv), lambda b, h, c: (b * nc + cidx(c), h)),
            pl.BlockSpec((T, dv), lambda b, h, c: (b * nc + cidx(c), kv + H + h)),
            pl.BlockSpec((1, dv), lambda b, h, c: (0, h)),
        ]
        args += [hprev, zq, norm_g]
    return pl.pallas_call(
        functools.partial(_mlstm_body, rev=rev, T=T, scale=dk ** -0.5, final=final),
        grid=(B, H, nc),
        in_specs=in_specs,
        out_specs=pl.BlockSpec((T, dv), lambda b, h, c: (b * nc + cidx(c), h)),
        out_shape=jax.ShapeDtypeStruct((B * L, H * dv), BF16 if final else F32),
        scratch_shapes=[pltpu.VMEM((dk, dv), F32), pltpu.VMEM((1, dk), F32), pltpu.VMEM((1, 1), F32)],
        compiler_params=_cparams("parallel", "parallel", "arbitrary"),
        name="mlstm_bwd" if rev else "mlstm_fwd",
    )(*args)


def _merge_body(yh_ref, ym_ref, ph_ref, pm_ref, gh_ref, gm_ref, o_ref):
    a = jnp.dot(yh_ref[...], ph_ref[...], preferred_element_type=F32)
    b = jnp.dot(ym_ref[...], pm_ref[...], preferred_element_type=F32)
    o_ref[...] = (jax.nn.sigmoid(gh_ref[...].astype(F32)) * a
                  + jax.nn.sigmoid(gm_ref[...].astype(F32)) * b).astype(o_ref.dtype)


def _merge(y_hy, y_ml, p_hy, p_ml, gates, tm=1024, tn=512):
    n, kh = y_hy.shape
    km = y_ml.shape[1]
    d = p_hy.shape[1]
    nj = d // tn
    return pl.pallas_call(
        _merge_body,
        grid=(n // tm, nj),
        in_specs=[pl.BlockSpec((tm, kh), lambda i, j: (i, 0)),
                  pl.BlockSpec((tm, km), lambda i, j: (i, 0)),
                  pl.BlockSpec((kh, tn), lambda i, j: (0, j)),
                  pl.BlockSpec((km, tn), lambda i, j: (0, j)),
                  pl.BlockSpec((tm, tn), lambda i, j: (i, j)),
                  pl.BlockSpec((tm, tn), lambda i, j: (i, nj + j))],
        out_specs=pl.BlockSpec((tm, tn), lambda i, j: (i, j)),
        out_shape=jax.ShapeDtypeStruct((n, d), BF16),
        compiler_params=_cparams("parallel", "arbitrary"),
        name="gated_merge",
    )(y_hy, y_ml, p_hy, p_ml, gates, gates)


def _layer_norm(x, g, b):
    mu = jnp.mean(x, axis=-1, keepdims=True)
    var = jnp.mean(jnp.square(x - mu), axis=-1, keepdims=True)
    return (x - mu) * lax.rsqrt(var + LN_EPS) * g + b


def _router_body(x_ref, mix_ref, lg_ref, lb_ref, w_ref, b_ref, h_ref, wout_ref, eout_ref, cnt_ref):
    G, PG = MOE_GROUPS, MOE_PER_GROUP

    @pl.when(pl.program_id(0) == 0)
    def _():
        cnt_ref[...] = jnp.zeros_like(cnt_ref)

    h = _layer_norm(DEEPNORM_ALPHA * x_ref[...] + mix_ref[...], lg_ref[...], lb_ref[...])
    h_ref[...] = h
    logits = jnp.dot(h, w_ref[...], precision=HIGHEST, preferred_element_type=F32) + b_ref[...]
    lane = lax.broadcasted_iota(jnp.int32, logits.shape, 1)
    ninf = -jnp.inf
    first = lambda mask: jnp.min(jnp.where(mask, lane, 2 * LANES), axis=1, keepdims=True)
    lg1 = jnp.where(lane < G, logits, ninf)
    m1 = jnp.max(lg1, axis=1, keepdims=True)
    g_sel = first(lg1 == m1)
    p_group = 1.0 / jnp.sum(jnp.exp(lg1 - m1), axis=1, keepdims=True)
    lo = G + g_sel * PG
    in_grp = jnp.logical_and(lane >= lo, lane < lo + PG)
    lg2 = jnp.where(in_grp, logits, ninf)
    m2 = jnp.max(lg2, axis=1, keepdims=True)
    e2 = jnp.exp(lg2 - m2)
    p2 = jnp.where(in_grp, e2 / jnp.sum(e2, axis=1, keepdims=True), -1.0)
    t1 = jnp.max(p2, axis=1, keepdims=True)
    j1 = first(p2 == t1)
    p2b = jnp.where(lane == j1, -1.0, p2)
    t2 = jnp.max(p2b, axis=1, keepdims=True)
    j2 = first(p2b == t2)
    tot = t1 + t2
    oh1, oh2 = lane == j1, lane == j2
    ohs = jnp.where(jnp.logical_or(oh1, oh2), 1.0, 0.0)
    tm = ohs.shape[0]
    earlier = (lax.broadcasted_iota(jnp.int32, (tm, tm), 1)
               < lax.broadcasted_iota(jnp.int32, (tm, tm), 0))
    base = cnt_ref[...] + jnp.dot(jnp.where(earlier, 1.0, 0.0).astype(BF16), ohs.astype(BF16),
                                  preferred_element_type=F32)
    r1 = jnp.sum(jnp.where(oh1, base, 0.0), axis=1, keepdims=True).astype(jnp.int32)
    r2 = jnp.sum(jnp.where(oh2, base, 0.0), axis=1, keepdims=True).astype(jnp.int32)
    cnt_ref[...] += jnp.sum(ohs, axis=0, keepdims=True)
    wout_ref[...] = jnp.where(lane == 0, p_group * (t1 / tot),
                              jnp.where(lane == 1, p_group * (t2 / tot), 0.0))
    eout_ref[...] = jnp.where(lane == 0, j1 - G, jnp.where(lane == 1, j2 - G,
                              jnp.where(lane == 2, r1, jnp.where(lane == 3, r2, 0))))


def _ln_router(x, mix, ln_g, ln_b, router_w1, router_b1, router_w2, router_b2, tm=256):
    n, d = x.shape
    ncol = MOE_GROUPS + MOE_GROUPS * MOE_PER_GROUP
    w = jnp.zeros((d, LANES), F32).at[:, :ncol].set(jnp.concatenate([router_w1, router_w2], axis=1))
    b = jnp.zeros((1, LANES), F32).at[0, :ncol].set(jnp.concatenate([router_b1, router_b2]))
    row = lambda width: pl.BlockSpec((tm, width), lambda i: (i, 0))
    const = lambda shape: pl.BlockSpec(shape, lambda i: (0, 0))
    return pl.pallas_call(
        _router_body,
        grid=(n // tm,),
        in_specs=[row(d), row(d), const((1, d)), const((1, d)), const((d, LANES)), const((1, LANES))],
        out_specs=[row(d), row(LANES), row(LANES), const((1, LANES))],
        out_shape=[jax.ShapeDtypeStruct((n, d), F32), jax.ShapeDtypeStruct((n, LANES), F32),
                   jax.ShapeDtypeStruct((n, LANES), jnp.int32), jax.ShapeDtypeStruct((1, LANES), F32)],
        compiler_params=_cparams("arbitrary"),
        name="ln_moe_router",
    )(x, mix, ln_g.reshape(1, d), ln_b.reshape(1, d), w, b)


def _row_copy(src_hbm, row, dst_vmem, r, sem):
    return pltpu.make_async_copy(src_hbm.at[pl.ds(row, 1), :], dst_vmem.at[pl.ds(r, 1), :], sem)


def _expert_body(e_ref, nb_ref, nused_ref, tok_ref, x_hbm, w1_ref, w3_ref, w2_ref, o_ref,
                 stage, xb, hb, sem, *, bm, rb, nhc):
    del e_ref, nused_ref
    i, c = pl.program_id(0), pl.program_id(1)
    nb = nb_ref[i]
    rmax = rb * bm

    def for_rows(grp, fn):
        def body(r, carry):
            fn(r, tok_ref[grp * rmax + r])
            return carry
        lax.fori_loop(0, nb_ref[grp] * bm, body, 0)

    @pl.when(jnp.logical_and(i == 0, c == 0))
    def _():
        for_rows(0, lambda r, t: _row_copy(x_hbm, t, stage, r, sem).start())

    @pl.when(jnp.logical_and(c == 0, nb > 0))
    def _():
        for_rows(i, lambda r, t: _row_copy(x_hbm, 0, stage, r, sem).wait())
        for b in range(rb):
            q = pl.ds(b * bm, bm)

            @pl.when(b < nb)
            def _():
                xb[q, :] = stage[q, :].astype(BF16)

            @pl.when(b >= nb)
            def _():
                xb[q, :] = jnp.zeros((bm, xb.shape[1]), BF16)

        @pl.when(i + 1 < pl.num_programs(0))
        def _():
            for_rows(i + 1, lambda r, t: _row_copy(x_hbm, t, stage, r, sem).start())

    def hidden(m):
        x = xb[0:m, :]
        a = jnp.dot(x, w1_ref[...].astype(BF16), preferred_element_type=F32)
        g = jnp.dot(x, w3_ref[...].astype(BF16), preferred_element_type=F32)
        hb[c, 0:m, :] = ((a * jax.nn.sigmoid(a)) * g).astype(BF16)

    def project(m):
        hc = hb.shape[2]
        y = jnp.dot(hb[0, 0:m, :], w2_ref[0:hc, :].astype(BF16), preferred_element_type=F32)
        for k in range(1, nhc):
            y += jnp.dot(hb[k, 0:m, :], w2_ref[k * hc:(k + 1) * hc, :].astype(BF16),
                         preferred_element_type=F32)
        o_ref[0:m, :] = y
        if m < rmax:
            o_ref[m:rmax, :] = jnp.zeros((rmax - m, o_ref.shape[1]), F32)

    small = (rb - 1) * bm
    for cond, m in ((jnp.logical_and(nb > 0, nb < rb), small), (nb == rb, rmax)):
        @pl.when(jnp.logical_and(cond, c < nhc))
        def _():
            hidden(m)

        @pl.when(jnp.logical_and(cond, c >= nhc))
        def _():
            project(m)

    @pl.when(jnp.logical_and(nb == 0, c >= nhc))
    def _():
        o_ref[...] = jnp.zeros_like(o_ref)


def _experts(x, sb_e, sb_nb, n_used, slot_tok, w1, w3, w2, bm, rb):
    n, d = x.shape
    e, _, hd = w1.shape
    n_sb = sb_e.shape[0]
    rmax = rb * bm
    hc, oc = min(MOE_HC, hd), min(MOE_OC, d)
    nhc, noc = hd // hc, d // oc

    def w13_map(i, c, se, nb, nu, tok):
        return se[i], 0, jnp.where(i < nu[0], jnp.minimum(c, nhc - 1), nhc - 1)

    def w2_map(i, c, se, nb, nu, tok):
        return se[i], 0, jnp.where(i < nu[0], jnp.clip(c - nhc, 0, noc - 1), noc - 1)

    def out_map(i, c, se, nb, nu, tok):
        writing = c >= nhc
        row = jnp.where(writing, i, jnp.maximum(i - 1, 0))
        col = jnp.where(writing, c - nhc, jnp.where(i == 0, 0, noc - 1))
        return row, col

    grid_spec = pltpu.PrefetchScalarGridSpec(
        num_scalar_prefetch=4,
        grid=(n_sb, nhc + noc),
        in_specs=[pl.BlockSpec(memory_space=pl.ANY),
                  pl.BlockSpec((None, d, hc), w13_map),
                  pl.BlockSpec((None, d, hc), w13_map),
                  pl.BlockSpec((None, hd, oc), w2_map)],
        out_specs=pl.BlockSpec((rmax, oc), out_map),
        scratch_shapes=[pltpu.VMEM((rmax, d), F32), pltpu.VMEM((rmax, d), BF16),
                        pltpu.VMEM((nhc, rmax, hc), BF16), pltpu.SemaphoreType.DMA(())],
    )
    return pl.pallas_call(
        functools.partial(_expert_body, bm=bm, rb=rb, nhc=nhc),
        grid_spec=grid_spec,
        out_shape=jax.ShapeDtypeStruct((n_sb * rmax, d), F32),
        compiler_params=_cparams("arbitrary", "arbitrary"),
        name="moe_experts",
    )(sb_e, sb_nb, n_used, slot_tok, x, w1, w3, w2)


def _combine_body(slot_ref, y_hbm, w_ref, x_ref, g_ref, b_ref, o_ref, buf, sem, *, tb):
    i = pl.program_id(0)

    def gather(tile, half):
        def issue(r, c):
            for kk in range(MOE_TOPK):
                _row_copy(y_hbm, slot_ref[(tile * tb + r) * MOE_TOPK + kk], buf.at[half, kk], r,
                          sem.at[half]).start()
            return c
        lax.fori_loop(0, tb, issue, 0)

    @pl.when(i == 0)
    def _():
        gather(0, 0)

    @pl.when(i + 1 < pl.num_programs(0))
    def _():
        gather(i + 1, (i + 1) % 2)

    half = i % 2

    def wait(r, c):
        for kk in range(MOE_TOPK):
            _row_copy(y_hbm, 0, buf.at[half, kk], r, sem.at[half]).wait()
        return c
    lax.fori_loop(0, tb, wait, 0)
    w = w_ref[...]
    y = w[:, 0:1] * buf[half, 0] + w[:, 1:2] * buf[half, 1]
    o_ref[...] = _layer_norm(DEEPNORM_ALPHA * x_ref[...] + y, g_ref[...], b_ref[...])


def _combine_ln(slot_of, yb, weights, x, g, b, tb):
    n, d = x.shape
    grid_spec = pltpu.PrefetchScalarGridSpec(
        num_scalar_prefetch=1,
        grid=(n // tb,),
        in_specs=[pl.BlockSpec(memory_space=pl.ANY),
                  pl.BlockSpec((tb, LANES), lambda i, s: (i, 0)),
                  pl.BlockSpec((tb, d), lambda i, s: (i, 0)),
                  pl.BlockSpec((1, d), lambda i, s: (0, 0)),
                  pl.BlockSpec((1, d), lambda i, s: (0, 0))],
        out_specs=pl.BlockSpec((tb, d), lambda i, s: (i, 0)),
        scratch_shapes=[pltpu.VMEM((2, MOE_TOPK, tb, d), F32), pltpu.SemaphoreType.DMA((2,))],
    )
    return pl.pallas_call(
        functools.partial(_combine_body, tb=tb),
        grid_spec=grid_spec,
        out_shape=jax.ShapeDtypeStruct((n, d), F32),
        compiler_params=_cparams("arbitrary"),
        name="moe_combine_ln",
    )(slot_of, yb, weights, x, g.reshape(1, d), b.reshape(1, d))


def _ln_moe_ln(x, mix, ln1_g, ln1_b, router_w1, router_b1, router_w2, router_b2, exp_w1, exp_w3, exp_w2,
               ln_g, ln_b):
    n, d = x.shape
    e = exp_w1.shape[0]
    bm = MOE_BM
    h1, weights, ids, cnt = _ln_router(x, mix, ln1_g, ln1_b, router_w1, router_b1, router_w2, router_b2)
    m = n * MOE_TOPK
    rb = MOE_RB
    rmax = rb * bm
    eid_f = ids[:, :MOE_TOPK].reshape(m)
    rank = ids[:, MOE_TOPK:2 * MOE_TOPK].reshape(m)
    counts = cnt[0, MOE_GROUPS:MOE_GROUPS + e].astype(jnp.int32)
    nblk_e = (counts + bm - 1) // bm
    ngrp_e = (nblk_e + rb - 1) // rb
    gend = jnp.cumsum(ngrp_e)
    gstart = gend - ngrp_e
    slot_of = ((gstart[eid_f] + rank // rmax) * rmax + rank % rmax).astype(jnp.int32)
    n_grp = -(-(-(-m // bm) + e * rb) // rb)
    tok_f = jnp.arange(m, dtype=jnp.int32) // MOE_TOPK
    slot_tok = jnp.zeros((n_grp * rmax,), jnp.int32).at[slot_of].set(tok_f)
    n_used = gend[-1:].astype(jnp.int32)
    gidx = jnp.arange(n_grp, dtype=jnp.int32)
    gcl = jnp.minimum(gidx, n_used[0] - 1)
    grp_e = jnp.minimum(jnp.sum(gend[None, :] <= gcl[:, None], axis=1), e - 1).astype(jnp.int32)
    local = gcl - gstart[grp_e]
    grp_nb = jnp.where(gidx < n_used[0], jnp.minimum(rb, nblk_e[grp_e] - local * rb), 0).astype(jnp.int32)
    yb = _experts(h1, grp_e, grp_nb, n_used, slot_tok, exp_w1, exp_w3, exp_w2, bm, rb)
    return _combine_ln(slot_of, yb, weights, h1, ln_g, ln_b, min(MOE_TB, n))


def kernel(x, w_in, hy_conv_w, hy_conv_b, hy_f_w1, hy_f_b1, hy_f_fr1, hy_f_w2, hy_f_b2, hy_f_fr2,
           hy_f_w3, hy_f_b3, hy_f_fr3, hy_f_wout, hy_bias, ml_gate_bias, ml_norm_g, p_hy, p_ml, w_out,
           ln1_g, ln1_b, router_w1, router_b1, router_w2, router_b2, exp_w1, exp_w3, exp_w2,
           ln2_g, ln2_b):
    B, L, D = x.shape
    N = B * L
    C = D // 2
    H = ML_HEADS
    dv = C // H
    dk = dv // 2
    col_q = (HY_ORDER + 1) * C
    col_if = col_q + 2 * H * dk + 2 * C
    col_gate = col_if + 4 * H

    xf = x.reshape(N, D)
    xb = xf.astype(BF16)
    w_hy = w_in[:, :col_q].astype(BF16)
    w_ml = w_in[:, col_q:col_if].astype(BF16)
    w_if = jnp.zeros((D, LANES), BF16).at[:, :4 * H].set(w_in[:, col_if:col_gate].astype(BF16))
    w_gt = w_in[:, col_gate:].astype(BF16)

    z_hy = _matmul(xb, w_hy, F32, 1024, 512).reshape(B, L, col_q)
    z_ml = _matmul(xb, w_ml, BF16, 1024, 512)
    z_if = _matmul(xb, w_if, F32, 1024, LANES)
    z_gt = _matmul(xb, w_gt, BF16, 1024, 512)

    tables, filter_tables = _dft_tables(L)
    kern = _hyena_filters(L, C, hy_f_w1, hy_f_b1, hy_f_fr1, hy_f_w2, hy_f_b2, hy_f_fr2,
                          hy_f_w3, hy_f_b3, hy_f_fr3, hy_f_wout, hy_bias)
    n2 = 2 * L // FFT_N1
    kfreq = _filter_fft(_row_permute(kern, FFT_N1, n2, F32), *filter_tables)
    u = _short_conv(z_hy, hy_conv_w, hy_conv_b, n2)
    v1 = _long_conv(u, 2, u, 0, kfreq, 0, tables, C)
    y_hy = _long_conv(v1, 0, u, 1, kfreq, 1, tables, C)
    y_hy = _row_permute(y_hy, n2, L // n2, BF16).reshape(N, C)

    g = z_if[:, :4 * H].reshape(B, L, 4, H)
    gates_c = g.transpose(0, 3, 1, 2)
    gates_r = g.transpose(0, 3, 2, 1)
    bias_c = ml_gate_bias.T.reshape(H, 1, 4)
    bias_r = ml_gate_bias.T.reshape(H, 4, 1)
    h_fwd = _mlstm(z_ml, gates_c, gates_r, bias_c, bias_r, B, L, dk, dv, rev=False)
    y_ml = _mlstm(z_ml, gates_c, gates_r, bias_c, bias_r, B, L, dk, dv, rev=True,
                  hprev=h_fwd, norm_g=ml_norm_g.reshape(1, C))

    merged = _merge(y_hy, y_ml, p_hy.astype(BF16), p_ml.astype(BF16), z_gt)
    mix = _matmul(merged, w_out.astype(BF16), F32, 1024, 512)
    out = _ln_moe_ln(xf, mix, ln1_g, ln1_b, router_w1, router_b1, router_w2, router_b2,
                     exp_w1, exp_w3, exp_w2, ln2_g, ln2_b)
    return out.reshape(B, L, D)
```

```python
import functools
import math

import jax
import jax.numpy as jnp
from jax import lax
from jax.experimental import pallas as pl
from jax.experimental.pallas import tpu as pltpu

F32 = jnp.float32
BF16 = jnp.bfloat16
HIGHEST = lax.Precision.HIGHEST

VMEM_LIMIT_BYTES = 56 * 1024 * 1024
LANES = 128

HY_ORDER = 2
HY_SHORT = 3
HY_POS_EMB = 33
HY_DECAY_TARGET = 1e-2
HY_FAST_DECAY = 0.3
HY_SLOW_DECAY = 1.5
HY_MOD_SHIFT = 0.05
ML_HEADS = 8
MOE_GROUPS = 8
MOE_PER_GROUP = 8
MOE_TOPK = 2
DEPTH = 1
DEEPNORM_ALPHA = (2.0 * DEPTH) ** 0.25
LN_EPS = 1e-5

FFT_N1 = 64
HY_CB = 256
FFT_G = 32
FFT_KC = 16
FFT_UNROLL_OUTER = 4
FFT_UNROLL_INNER = 4
ML_CHUNK = 256
ML_CHUNKS_PER_STEP = 4
MOE_BM = 128
MOE_RB = 5
MOE_KC = 1024
MOE_OC = 1024
MOE_TB = 256


def _cparams(*sem):
    return pltpu.CompilerParams(dimension_semantics=sem, vmem_limit_bytes=VMEM_LIMIT_BYTES)


def _mm_body(a_ref, b_ref, o_ref):
    o_ref[...] = jnp.dot(a_ref[...], b_ref[...], preferred_element_type=F32).astype(o_ref.dtype)


def _matmul(a, b, out_dtype, tm, tn):
    m, k = a.shape
    _, n = b.shape
    assert m % tm == 0 and n % tn == 0
    return pl.pallas_call(
        _mm_body,
        grid=(m // tm, n // tn),
        in_specs=[pl.BlockSpec((tm, k), lambda i, j: (i, 0)),
                  pl.BlockSpec((k, tn), lambda i, j: (0, j))],
        out_specs=pl.BlockSpec((tm, tn), lambda i, j: (i, j)),
        out_shape=jax.ShapeDtypeStruct((m, n), out_dtype),
        compiler_params=_cparams("parallel", "arbitrary"),
        name="proj_matmul",
    )(a, b)


def _permute_pitch(outer):
    return outer + 8


def _permute_rows(get_rows, tmp_ref, dst_ref, inner, outer):
    pitch = _permute_pitch(outer)
    for b in range(inner):
        tmp_ref[0, b * pitch:b * pitch + outer, :] = get_rows(b * outer, (b + 1) * outer)

    def it(a, c):
        rows = tmp_ref[0, pl.ds(a, inner, stride=pitch), :]
        dst_ref[0, pl.ds(pl.multiple_of(a * inner, inner), inner), :] = rows.astype(dst_ref.dtype)
        return c
    lax.fori_loop(0, outer, it, 0, unroll=4)


def _short_conv_body(z_ref, w_ref, b_ref, o_ref, tmp, *, n2):
    z = z_ref[0]
    L = z.shape[0]
    row = lax.broadcasted_iota(jnp.int32, z.shape, 0)
    prev = jnp.where(row == 0, 0.0, pltpu.roll(z, 1, 0))
    nxt = jnp.where(row == L - 1, 0.0, pltpu.roll(z, L - 1, 0))
    u = b_ref[...] + prev * w_ref[0:1, :] + z * w_ref[1:2, :] + nxt * w_ref[2:3, :]
    _permute_rows(lambda lo, hi: u[lo:hi], tmp, o_ref, L // n2, n2)


def _short_conv(z, w, b, n2, cb=LANES):
    B, L, C = z.shape
    return pl.pallas_call(
        functools.partial(_short_conv_body, n2=n2),
        grid=(B, C // cb),
        in_specs=[pl.BlockSpec((1, L, cb), lambda i, j: (i, 0, j)),
                  pl.BlockSpec((HY_SHORT, cb), lambda i, j: (0, j)),
                  pl.BlockSpec((1, cb), lambda i, j: (0, j))],
        out_specs=pl.BlockSpec((1, L, cb), lambda i, j: (i, 0, j)),
        out_shape=jax.ShapeDtypeStruct((B, L, C), F32),
        scratch_shapes=[pltpu.VMEM((1, (L // n2) * _permute_pitch(n2), cb), F32)],
        compiler_params=_cparams("parallel", "parallel"),
        name="hy_short_conv",
    )(z, w, b.reshape(1, C))


def _row_permute_body(x_ref, o_ref, tmp, *, inner, outer):
    _permute_rows(lambda lo, hi: x_ref[0, lo:hi, :], tmp, o_ref, inner, outer)


def _row_permute(x, inner, outer, out_dtype, cb=LANES):
    A, R, C = x.shape
    assert R == inner * outer
    cb = min(cb, C)
    return pl.pallas_call(
        functools.partial(_row_permute_body, inner=inner, outer=outer),
        grid=(A, C // cb),
        in_specs=[pl.BlockSpec((1, R, cb), lambda i, j: (i, 0, j))],
        out_specs=pl.BlockSpec((1, R, cb), lambda i, j: (i, 0, j)),
        out_shape=jax.ShapeDtypeStruct((A, R, C), out_dtype),
        scratch_shapes=[pltpu.VMEM((1, inner * _permute_pitch(outer), cb), F32)],
        compiler_params=_cparams("parallel", "parallel"),
        name="hy_row_permute",
    )(x)


def _filter_hidden_body(w1_ref, b1_ref, fr1_ref, w2_ref, b2_ref, fr2_ref, w3_ref, b3_ref, fr3_ref,
                        o_ref, *, L, rows):
    i = pl.program_id(0)
    n = i * rows + lax.broadcasted_iota(jnp.int32, (rows, 1), 0)
    pos = jnp.where(n < L, n, 2 * L - n).astype(F32)
    t = pos / (L - 1.0)
    w = (2.0 * math.pi / L) * pos
    lane = lax.broadcasted_iota(jnp.int32, (1, LANES), 1)
    bands = (HY_POS_EMB - 1) // 2
    band = jnp.where(lane <= bands, lane - 1, lane - 1 - bands).astype(F32)
    freq = 1e-4 + band * ((bands - 1 - 1e-4) / (bands - 1))
    ang = w * freq
    feats = jnp.where(lane == 0, t,
                      jnp.where(lane <= bands, jnp.cos(ang),
                                jnp.where(lane <= 2 * bands, -jnp.sin(ang), 0.0)))
    h = jnp.sin(fr1_ref[...] * (jnp.dot(feats, w1_ref[...], precision=HIGHEST,
                                        preferred_element_type=F32) + b1_ref[...]))
    h = jnp.sin(fr2_ref[...] * (jnp.dot(h, w2_ref[...], precision=HIGHEST,
                                        preferred_element_type=F32) + b2_ref[...]))
    h = jnp.sin(fr3_ref[...] * (jnp.dot(h, w3_ref[...], precision=HIGHEST,
                                        preferred_element_type=F32) + b3_ref[...]))
    o_ref[...] = h


def _filter_out_body(h_ref, wout_ref, delta_ref, bias_ref, o_ref, *, L):
    d = pl.program_id(1)
    h = jnp.dot(h_ref[...], wout_ref[0], precision=HIGHEST, preferred_element_type=F32)
    r = lax.broadcasted_iota(jnp.int32, (L, 1), 0)
    pos = jnp.where(d == 0, r, L - r).astype(F32)
    t = pos / (L - 1.0)
    window = jnp.exp(-t * delta_ref[...]) + HY_MOD_SHIFT
    first = r == 0
    tap = jnp.where(jnp.logical_and(d == 1, first), 0.0, h * window)
    o_ref[0] = tap + jnp.where(jnp.logical_and(d == 0, first), bias_ref[...], 0.0)


def _hyena_filters(L, C, f_w1, f_b1, f_fr1, f_w2, f_b2, f_fr2, f_w3, f_b3, f_fr3, f_wout, bias, cb=512):
    fh = f_w2.shape[0]
    rows = 1024 if (2 * L) % 1024 == 0 else 2 * L
    w1p = jnp.zeros((LANES, fh), F32).at[:HY_POS_EMB].set(f_w1)
    vec = lambda a: a.reshape(1, fh)
    full = lambda shape: pl.BlockSpec(shape, lambda i: (0,) * len(shape))
    hid = pl.pallas_call(
        functools.partial(_filter_hidden_body, L=L, rows=rows),
        grid=(2 * L // rows,),
        in_specs=[full((LANES, fh)), full((1, fh)), full((1, fh)),
                  full((fh, fh)), full((1, fh)), full((1, fh)),
                  full((fh, fh)), full((1, fh)), full((1, fh))],
        out_specs=pl.BlockSpec((rows, fh), lambda i: (i, 0)),
        out_shape=jax.ShapeDtypeStruct((2 * L, fh), F32),
        compiler_params=_cparams("parallel"),
        name="hy_filter_hidden",
    )(w1p, vec(f_b1), vec(f_fr1), f_w2, vec(f_b2), vec(f_fr2), f_w3, vec(f_b3), vec(f_fr3))
    wout = f_wout.reshape(fh, 2, HY_ORDER, C).transpose(1, 2, 0, 3).reshape(2 * HY_ORDER, fh, C)
    deltas = jnp.abs(jnp.linspace(math.log(HY_DECAY_TARGET) / HY_SLOW_DECAY,
                                  math.log(HY_DECAY_TARGET) / HY_FAST_DECAY, C, dtype=F32))
    cb = min(cb, C)
    return pl.pallas_call(
        functools.partial(_filter_out_body, L=L),
        grid=(HY_ORDER, 2, C // cb),
        in_specs=[pl.BlockSpec((L, fh), lambda o, d, j: (d, 0)),
                  pl.BlockSpec((1, fh, cb), lambda o, d, j: (d * HY_ORDER + o, 0, j)),
                  pl.BlockSpec((1, cb), lambda o, d, j: (0, j)),
                  pl.BlockSpec((None, 1, cb), lambda o, d, j: (o, 0, j))],
        out_specs=pl.BlockSpec((1, L, cb), lambda o, d, j: (o, d, j)),
        out_shape=jax.ShapeDtypeStruct((HY_ORDER, 2 * L, C), F32),
        compiler_params=_cparams("parallel", "parallel", "parallel"),
        name="hy_filter_out",
    )(hid, wout, deltas.reshape(1, C), bias.reshape(HY_ORDER, 1, C))


def _dft_tables(L):
    N = 2 * L
    N1 = FFT_N1
    N2 = N // N1
    h = N1 // 2
    n2 = jnp.arange(N2, dtype=jnp.int32)[:, None, None]
    k1 = jnp.arange(N1, dtype=jnp.int32)[None, :, None]
    n1 = jnp.arange(N1, dtype=jnp.int32)[None, None, :]
    ph = (k1 * (N2 * n1 + n2)) % N
    ang = ph.astype(F32) * (-2.0 * math.pi / N)
    mr, mi = jnp.cos(ang), jnp.sin(ang)
    mrp, mip = mr[:, :, :h], mi[:, :, :h]
    t1 = jnp.concatenate([jnp.concatenate([mrp, -mip], axis=2),
                          jnp.concatenate([mip, mrp], axis=2)], axis=1)
    mrt, mit = jnp.swapaxes(mrp, 1, 2) / N, jnp.swapaxes(mip, 1, 2) / N
    t1i = jnp.concatenate([jnp.concatenate([mrt, mit], axis=2),
                           jnp.concatenate([-mit, mrt], axis=2)], axis=1)
    t1f = jnp.concatenate([mr, mi], axis=1)
    a = jnp.arange(N2, dtype=jnp.int32)
    ang2 = ((a[:, None] * a[None, :]) % N2).astype(F32) * (-2.0 * math.pi / N2)
    fr, fi = jnp.cos(ang2), jnp.sin(ang2)
    t2 = jnp.concatenate([jnp.concatenate([fr, -fi], axis=1),
                          jnp.concatenate([fi, fr], axis=1)], axis=0)
    t2i = jnp.concatenate([jnp.concatenate([fr, fi], axis=1),
                           jnp.concatenate([-fi, fr], axis=1)], axis=0)
    conv_tables = tuple(t.astype(BF16) for t in (t1, t1i, t2, t2i))
    filter_tables = (_hi_lo_rows(t1f), _hi_lo_rows(t2))
    return conv_tables, filter_tables


def _split_bf16(d):
    hi = d.astype(BF16)
    return hi, (d - hi.astype(F32)).astype(BF16)


def _hi_lo_rows(t):
    hi, lo = _split_bf16(t)
    return jnp.concatenate([hi, lo], axis=-2)


def _ld_lanes(ref, rows):
    return jnp.concatenate([ref[i, rows, :] for i in range(ref.shape[0])], axis=1)


def _st_lanes(ref, rows, val):
    for i in range(ref.shape[0]):
        ref[i, rows, :] = val[:, i * LANES:(i + 1) * LANES]


def _dot3(t, d, m):
    d_hi, d_lo = _split_bf16(d)
    y = jnp.dot(t, d_hi, preferred_element_type=F32)
    return y[:m] + y[m:] + jnp.dot(t[:m], d_lo, preferred_element_type=F32)


def _dot1(t, d):
    return jnp.dot(t, d.astype(BF16), preferred_element_type=F32)


def _work_pitch(N2):
    return N2 + 8


def _fft_steps(N1, N2):
    return min(FFT_G, N2), min(FFT_KC, N1)


def _filter_fft_body(k_ref, t1f_ref, t2_ref, o_ref, sr, si, *, N1, N2, G, KC):
    s = pl.program_id(2)
    sa = N2 // G
    P = _work_pitch(N2)

    @pl.when(s < sa)
    def _():
        def it(r, c):
            n2 = s * G + r
            rows = k_ref[pl.ds(pl.multiple_of(r * N1, N1), N1), :]
            a = _dot3(t1f_ref[n2], rows, 2 * N1)
            _st_lanes(sr, pl.ds(n2, N1, stride=P), a[:N1])
            _st_lanes(si, pl.ds(n2, N1, stride=P), a[N1:])
            return c
        lax.fori_loop(0, G, it, 0, unroll=FFT_UNROLL_OUTER)

    @pl.when(s >= sa)
    def _():
        t2 = t2_ref[...]

        def it(kk, c):
            r0 = pl.multiple_of(((s - sa) * KC + kk) * P, 8)
            q0 = pl.multiple_of(kk * N2, N2)
            x = _dot3(t2, jnp.concatenate([_ld_lanes(sr, pl.ds(r0, N2)), _ld_lanes(si, pl.ds(r0, N2))],
                                          axis=0), 2 * N2)
            o_ref[0, pl.ds(q0, N2), :] = x[:N2]
            o_ref[1, pl.ds(q0, N2), :] = x[N2:]
            return c
        lax.fori_loop(0, KC, it, 0, unroll=FFT_UNROLL_INNER)


def _filter_fft(kern, t1f, t2):
    O, N, C = kern.shape
    N1, N2 = FFT_N1, N // FFT_N1
    G, KC = _fft_steps(N1, N2)
    sa, sb = N2 // G, N1 // KC
    cb = min(HY_CB, C)
    full = lambda shape: pl.BlockSpec(shape, lambda o, j, s: (0,) * len(shape),
                                      pipeline_mode=pl.Buffered(1))
    return pl.pallas_call(
        functools.partial(_filter_fft_body, N1=N1, N2=N2, G=G, KC=KC),
        grid=(O, C // cb, sa + sb),
        in_specs=[pl.BlockSpec((None, G * N1, cb), lambda o, j, s: (o, jnp.minimum(s, sa - 1), j)),
                  full(t1f.shape), full(t2.shape)],
        out_specs=pl.BlockSpec((None, 2, KC * N2, cb),
                               lambda o, j, s: (o, 0, jnp.maximum(s - sa, 0), j)),
        out_shape=jax.ShapeDtypeStruct((O, 2, N, C), F32),
        scratch_shapes=[pltpu.VMEM((cb // LANES, N1 * _work_pitch(N2), LANES), F32)] * 2,
        compiler_params=_cparams("parallel", "parallel", "arbitrary"),
        name="hy_filter_fft",
    )(kern, t1f, t2)


def _long_conv_body(v_ref, g_ref, kf_ref, t1_ref, t1i_ref, t2_ref, t2i_ref, o_ref, sr, si,
                    *, N1, N2, G, KC):
    s = pl.program_id(2)
    sa, sb = N2 // G, N1 // KC
    h = N1 // 2
    P = _work_pitch(N2)

    @pl.when(s < sa)
    def _():
        def it(r, c):
            n2 = s * G + r
            q = pl.ds(pl.multiple_of(r * h, h), h)
            d = jnp.concatenate([v_ref[0, q, :], v_ref[1, q, :]], axis=0)
            a = _dot1(t1_ref[n2], d)
            _st_lanes(sr, pl.ds(n2, N1, stride=P), a[:N1])
            _st_lanes(si, pl.ds(n2, N1, stride=P), a[N1:])
            return c
        lax.fori_loop(0, G, it, 0, unroll=FFT_UNROLL_OUTER)

    @pl.when(jnp.logical_and(s >= sa, s < sa + sb))
    def _():
        t2, t2i = t2_ref[...], t2i_ref[...]

        def it(kk, c):
            r0 = pl.multiple_of(((s - sa) * KC + kk) * P, 8)
            q0 = pl.multiple_of(kk * N2, N2)
            x = _dot1(t2, jnp.concatenate([_ld_lanes(sr, pl.ds(r0, N2)), _ld_lanes(si, pl.ds(r0, N2))],
                                          axis=0))
            xr, xi = x[:N2], x[N2:]
            kr = kf_ref[0, pl.ds(q0, N2), :]
            ki = kf_ref[1, pl.ds(q0, N2), :]
            p = jnp.concatenate([xr * kr - xi * ki, xr * ki + xi * kr], axis=0)
            y = _dot1(t2i, p)
            _st_lanes(sr, pl.ds(r0, N2), y[:N2])
            _st_lanes(si, pl.ds(r0, N2), y[N2:])
            return c
        lax.fori_loop(0, KC, it, 0, unroll=FFT_UNROLL_INNER)

    @pl.when(s >= sa + sb)
    def _():
        def it(r, c):
            n2 = (s - sa - sb) * G + r
            q = pl.ds(pl.multiple_of(r * h, h), h)
            d = jnp.concatenate([_ld_lanes(sr, pl.ds(n2, N1, stride=P)),
                                 _ld_lanes(si, pl.ds(n2, N1, stride=P))], axis=0)
            y = _dot1(t1i_ref[n2], d)
            for b in range(2):
                o_ref[b, q, :] = g_ref[b, q, :] * y[b * h:(b + 1) * h]
            return c
        lax.fori_loop(0, G, it, 0, unroll=FFT_UNROLL_OUTER)


def _long_conv(v_arr, v_off, g_arr, g_off, kfreq, order, tables, C):
    B, L, _ = v_arr.shape
    t1, t1i, t2, t2i = tables
    N = 2 * L
    N1, N2 = FFT_N1, N // FFT_N1
    G, KC = _fft_steps(N1, N2)
    sa, sb = N2 // G, N1 // KC
    h = N1 // 2
    cb = min(HY_CB, C)
    nj = C // cb
    full = lambda shape: pl.BlockSpec(shape, lambda j, p, s: (0,) * len(shape),
                                      pipeline_mode=pl.Buffered(1))
    last = lambda s: jnp.clip(s - sa - sb, 0, sa - 1)
    return pl.pallas_call(
        functools.partial(_long_conv_body, N1=N1, N2=N2, G=G, KC=KC),
        grid=(nj, B // 2, 2 * sa + sb),
        in_specs=[pl.BlockSpec((2, G * h, cb), lambda j, p, s: (p, jnp.minimum(s, sa - 1), v_off * nj + j)),
                  pl.BlockSpec((2, G * h, cb), lambda j, p, s: (p, last(s), g_off * nj + j)),
                  pl.BlockSpec((None, 2, KC * N2, cb),
                               lambda j, p, s: (order, 0, jnp.clip(s - sa, 0, sb - 1), j)),
                  full(t1.shape), full(t1i.shape), full(t2.shape), full(t2i.shape)],
        out_specs=pl.BlockSpec((2, G * h, cb), lambda j, p, s: (p, last(s), j)),
        out_shape=jax.ShapeDtypeStruct((B, L, C), F32),
        scratch_shapes=[pltpu.VMEM((cb // LANES, N1 * _work_pitch(N2), LANES), F32)] * 2,
        compiler_params=_cparams("parallel", "arbitrary", "arbitrary"),
        name="hy_long_conv",
    )(v_arr, g_arr, kfreq, t1, t1i, t2, t2i)


def _log_sigmoid(x):
    return jnp.minimum(x, 0.0) - jnp.log1p(jnp.exp(-jnp.abs(x)))


def _mlstm_body(*refs, rev, T, nsub, scale, final):
    if final:
        (q_ref, k_ref, v_ref, gc_ref, gr_ref, bc_ref, br_ref, hprev_ref, o_ref, ng_ref,
         out_ref, c_s, n_s, m_s) = refs
    else:
        q_ref, k_ref, v_ref, gc_ref, gr_ref, bc_ref, br_ref, out_ref, c_s, n_s, m_s = refs

    @pl.when(pl.program_id(2) == 0)
    def _():
        c_s[...] = jnp.zeros_like(c_s)
        n_s[...] = jnp.zeros_like(n_s)
        m_s[...] = jnp.zeros_like(m_s)

    gi = 2 if rev else 0
    row = lax.broadcasted_iota(jnp.int32, (T, T), 0)
    col = lax.broadcasted_iota(jnp.int32, (T, T), 1)
    valid = (col >= row) if rev else (col <= row)
    valid_t = (row >= col) if rev else (row <= col)

    def chunk(j):
        rows = slice(j * T, (j + 1) * T)
        gc = gc_ref[rows, :] + bc_ref[...]
        gr = gr_ref[:, rows] + br_ref[...]
        li_c, lf_c = gc[:, gi:gi + 1], _log_sigmoid(gc[:, gi + 1:gi + 2])
        li_r, lf_r = gr[gi:gi + 1, :], _log_sigmoid(gr[gi + 1:gi + 2, :])
        b_c = jnp.sum(jnp.where(valid, lf_r, 0.0), axis=1, keepdims=True)
        b_r = jnp.sum(jnp.where(valid_t, lf_c, 0.0), axis=0, keepdims=True)
        m = m_s[...]
        d = jnp.where(valid, b_c - b_r + li_r, -jnp.inf)
        inter = b_c + m
        m_t = jnp.maximum(inter, jnp.max(d, axis=1, keepdims=True))
        q, k, v = q_ref[rows, :], k_ref[rows, :], v_ref[rows, :]
        qk = lax.dot_general(q, k, (((1,), (1,)), ((), ())), preferred_element_type=F32)
        s = qk * scale * jnp.exp(d - m_t)
        w_inter = jnp.exp(inter - m_t)
        qc = jnp.dot(q, c_s[...].astype(BF16), preferred_element_type=F32) * scale
        num = jnp.dot(s.astype(BF16), v, preferred_element_type=F32) + w_inter * qc
        qn = jnp.sum(q.astype(F32) * n_s[...], axis=1, keepdims=True) * scale
        den = jnp.sum(s, axis=1, keepdims=True) + w_inter * qn
        hout = num / jnp.maximum(jnp.abs(den), jnp.exp(-m_t))

        b_last = b_c[0:1, :] if rev else b_c[T - 1:T, :]
        w_c = b_last - b_c + li_c
        m_new = jnp.maximum(b_last + m, jnp.max(w_c, axis=0, keepdims=True))
        kw = k.astype(F32) * jnp.exp(w_c - m_new)
        decay = jnp.exp(b_last + m - m_new)
        c_s[...] = decay * c_s[...] + lax.dot_general(
            kw.astype(BF16), v, (((0,), (0,)), ((), ())), preferred_element_type=F32)
        n_s[...] = decay * n_s[...] + jnp.sum(kw, axis=0, keepdims=True)
        m_s[...] = m_new

        if final:
            hsum = hout + hprev_ref[rows, :]
            mu = jnp.mean(hsum, axis=1, keepdims=True)
            var = jnp.mean(jnp.square(hsum - mu), axis=1, keepdims=True)
            hn = (hsum - mu) * lax.rsqrt(var + LN_EPS) * ng_ref[...]
            out_ref[rows, :] = (jax.nn.sigmoid(o_ref[rows, :].astype(F32)) * hn).astype(out_ref.dtype)
        else:
            out_ref[rows, :] = hout

    for j in (reversed(range(nsub)) if rev else range(nsub)):
        chunk(j)


def _mlstm(zq, gates_c, gates_r, bias_c, bias_r, B, L, dk, dv, rev, hprev=None, norm_g=None):
    H = ML_HEADS
    T = min(ML_CHUNK, L)
    nsub = min(ML_CHUNKS_PER_STEP, L // T)
    tb = T * nsub
    nc = L // tb
    final = hprev is not None
    cidx = (lambda c: nc - 1 - c) if rev else (lambda c: c)
    kq, kv = H * dk // dk, (2 * H * dk) // dv
    in_specs = [
        pl.BlockSpec((tb, dk), lambda b, h, c: (b * nc + cidx(c), h)),
        pl.BlockSpec((tb, dk), lambda b, h, c: (b * nc + cidx(c), kq + h)),
        pl.BlockSpec((tb, dv), lambda b, h, c: (b * nc + cidx(c), kv + h)),
        pl.BlockSpec((None, None, tb, 4), lambda b, h, c: (b, h, cidx(c), 0)),
        pl.BlockSpec((None, None, 4, tb), lambda b, h, c: (b, h, 0, cidx(c))),
        pl.BlockSpec((None, 1, 4), lambda b, h, c: (h, 0, 0)),
        pl.BlockSpec((None, 4, 1), lambda b, h, c: (h, 0, 0)),
    ]
    args = [zq, zq, zq, gates_c, gates_r, bias_c, bias_r]
    if final:
        in_specs += [
            pl.BlockSpec((tb, dv), lambda b, h, c: (b * nc + cidx(c), h)),
            pl.BlockSpec((tb, dv), lambda b, h, c: (b * nc + cidx(c), kv + H + h)),
            pl.BlockSpec((1, dv), lambda b, h, c: (0, h)),
        ]
        args += [hprev, zq, norm_g]
    return pl.pallas_call(
        functools.partial(_mlstm_body, rev=rev, T=T, nsub=nsub, scale=dk ** -0.5, final=final),
        grid=(B, H, nc),
        in_specs=in_specs,
        out_specs=pl.BlockSpec((tb, dv), lambda b, h, c: (b * nc + cidx(c), h)),
        out_shape=jax.ShapeDtypeStruct((B * L, H * dv), BF16 if final else F32),
        scratch_shapes=[pltpu.VMEM((dk, dv), F32), pltpu.VMEM((1, dk), F32), pltpu.VMEM((1, 1), F32)],
        compiler_params=_cparams("parallel", "parallel", "arbitrary"),
        name="mlstm_bwd" if rev else "mlstm_fwd",
    )(*args)


def _merge_body(yh_ref, ym_ref, ph_ref, pm_ref, gh_ref, gm_ref, o_ref):
    a = jnp.dot(yh_ref[...], ph_ref[...], preferred_element_type=F32)
    b = jnp.dot(ym_ref[...], pm_ref[...], preferred_element_type=F32)
    o_ref[...] = (jax.nn.sigmoid(gh_ref[...].astype(F32)) * a
                  + jax.nn.sigmoid(gm_ref[...].astype(F32)) * b).astype(o_ref.dtype)


def _merge(y_hy, y_ml, p_hy, p_ml, gates, tm=1024, tn=512):
    n, kh = y_hy.shape
    km = y_ml.shape[1]
    d = p_hy.shape[1]
    nj = d // tn
    return pl.pallas_call(
        _merge_body,
        grid=(n // tm, nj),
        in_specs=[pl.BlockSpec((tm, kh), lambda i, j: (i, 0)),
                  pl.BlockSpec((tm, km), lambda i, j: (i, 0)),
                  pl.BlockSpec((kh, tn), lambda i, j: (0, j)),
                  pl.BlockSpec((km, tn), lambda i, j: (0, j)),
                  pl.BlockSpec((tm, tn), lambda i, j: (i, j)),
                  pl.BlockSpec((tm, tn), lambda i, j: (i, nj + j))],
        out_specs=pl.BlockSpec((tm, tn), lambda i, j: (i, j)),
        out_shape=jax.ShapeDtypeStruct((n, d), BF16),
        compiler_params=_cparams("parallel", "arbitrary"),
        name="gated_merge",
    )(y_hy, y_ml, p_hy, p_ml, gates, gates)


def _layer_norm(x, g, b):
    mu = jnp.mean(x, axis=-1, keepdims=True)
    var = jnp.mean(jnp.square(x - mu), axis=-1, keepdims=True)
    return (x - mu) * lax.rsqrt(var + LN_EPS) * g + b


def _router_body(x_ref, mix_ref, lg_ref, lb_ref, w_ref, b_ref, h_ref, wout_ref, eout_ref, cnt_ref):
    G, PG = MOE_GROUPS, MOE_PER_GROUP

    @pl.when(pl.program_id(0) == 0)
    def _():
        cnt_ref[...] = jnp.zeros_like(cnt_ref)

    h = _layer_norm(DEEPNORM_ALPHA * x_ref[...] + mix_ref[...], lg_ref[...], lb_ref[...])
    h_ref[...] = h
    logits = jnp.dot(h, w_ref[...], precision=HIGHEST, preferred_element_type=F32) + b_ref[...]
    lane = lax.broadcasted_iota(jnp.int32, logits.shape, 1)
    ninf = -jnp.inf
    first = lambda mask: jnp.min(jnp.where(mask, lane, 2 * LANES), axis=1, keepdims=True)
    lg1 = jnp.where(lane < G, logits, ninf)
    m1 = jnp.max(lg1, axis=1, keepdims=True)
    g_sel = first(lg1 == m1)
    p_group = 1.0 / jnp.sum(jnp.exp(lg1 - m1), axis=1, keepdims=True)
    lo = G + g_sel * PG
    in_grp = jnp.logical_and(lane >= lo, lane < lo + PG)
    lg2 = jnp.where(in_grp, logits, ninf)
    m2 = jnp.max(lg2, axis=1, keepdims=True)
    e2 = jnp.exp(lg2 - m2)
    p2 = jnp.where(in_grp, e2 / jnp.sum(e2, axis=1, keepdims=True), -1.0)
    t1 = jnp.max(p2, axis=1, keepdims=True)
    j1 = first(p2 == t1)
    p2b = jnp.where(lane == j1, -1.0, p2)
    t2 = jnp.max(p2b, axis=1, keepdims=True)
    j2 = first(p2b == t2)
    tot = t1 + t2
    oh1, oh2 = lane == j1, lane == j2
    ohs = jnp.where(jnp.logical_or(oh1, oh2), 1.0, 0.0)
    tm = ohs.shape[0]
    earlier = (lax.broadcasted_iota(jnp.int32, (tm, tm), 1)
               < lax.broadcasted_iota(jnp.int32, (tm, tm), 0))
    base = cnt_ref[...] + jnp.dot(jnp.where(earlier, 1.0, 0.0).astype(BF16), ohs.astype(BF16),
                                  preferred_element_type=F32)
    r1 = jnp.sum(jnp.where(oh1, base, 0.0), axis=1, keepdims=True).astype(jnp.int32)
    r2 = jnp.sum(jnp.where(oh2, base, 0.0), axis=1, keepdims=True).astype(jnp.int32)
    cnt_ref[...] += jnp.sum(ohs, axis=0, keepdims=True)
    wout_ref[...] = jnp.where(lane == 0, p_group * (t1 / tot),
                              jnp.where(lane == 1, p_group * (t2 / tot), 0.0))
    eout_ref[...] = jnp.where(lane == 0, j1 - G, jnp.where(lane == 1, j2 - G,
                              jnp.where(lane == 2, r1, jnp.where(lane == 3, r2, 0))))


def _ln_router(x, mix, ln_g, ln_b, router_w1, router_b1, router_w2, router_b2, tm=256):
    n, d = x.shape
    ncol = MOE_GROUPS + MOE_GROUPS * MOE_PER_GROUP
    w = jnp.zeros((d, LANES), F32).at[:, :ncol].set(jnp.concatenate([router_w1, router_w2], axis=1))
    b = jnp.zeros((1, LANES), F32).at[0, :ncol].set(jnp.concatenate([router_b1, router_b2]))
    row = lambda width: pl.BlockSpec((tm, width), lambda i: (i, 0))
    const = lambda shape: pl.BlockSpec(shape, lambda i: (0, 0))
    return pl.pallas_call(
        _router_body,
        grid=(n // tm,),
        in_specs=[row(d), row(d), const((1, d)), const((1, d)), const((d, LANES)), const((1, LANES))],
        out_specs=[row(d), row(LANES), row(LANES), const((1, LANES))],
        out_shape=[jax.ShapeDtypeStruct((n, d), F32), jax.ShapeDtypeStruct((n, LANES), F32),
                   jax.ShapeDtypeStruct((n, LANES), jnp.int32), jax.ShapeDtypeStruct((1, LANES), F32)],
        compiler_params=_cparams("arbitrary"),
        name="ln_moe_router",
    )(x, mix, ln_g.reshape(1, d), ln_b.reshape(1, d), w, b)


def _row_copy(src_hbm, row, dst_vmem, r, sem):
    return pltpu.make_async_copy(src_hbm.at[pl.ds(row, 1), :], dst_vmem.at[pl.ds(r, 1), :], sem)


def _expert_body(e_ref, nb_ref, nused_ref, tok_ref, x_hbm, w1_ref, w3_ref, w2_ref, o_ref,
                 stage, xb, acc_a, acc_g, hb, sem, *, bm, rb, nkc):
    del e_ref, nused_ref
    i, c = pl.program_id(0), pl.program_id(1)
    nb = nb_ref[i]
    rmax = rb * bm
    tk = xb.shape[2]

    def for_rows(grp, fn):
        def block(b, carry):
            def body(r, carry2):
                row = b * bm + r
                fn(row, tok_ref[grp * rmax + row])
                return carry2
            return lax.fori_loop(0, bm, body, carry, unroll=8)
        lax.fori_loop(0, nb_ref[grp], block, 0)

    @pl.when(jnp.logical_and(i == 0, c == 0))
    def _():
        for_rows(0, lambda r, t: _row_copy(x_hbm, t, stage, r, sem).start())

    @pl.when(jnp.logical_and(c == 0, nb > 0))
    def _():
        for_rows(i, lambda r, t: _row_copy(x_hbm, 0, stage, r, sem).wait())
        for b in range(rb):
            q = pl.ds(b * bm, bm)

            @pl.when(b < nb)
            def _():
                for k in range(nkc):
                    xb[k, q, :] = stage[q, k * tk:(k + 1) * tk].astype(BF16)

            @pl.when(b >= nb)
            def _():
                for k in range(nkc):
                    xb[k, q, :] = jnp.zeros((bm, tk), BF16)

        @pl.when(i + 1 < pl.num_programs(0))
        def _():
            for_rows(i + 1, lambda r, t: _row_copy(x_hbm, t, stage, r, sem).start())

    def hidden(m):
        x = xb[c, 0:m, :]
        a = jnp.dot(x, w1_ref[...].astype(BF16), preferred_element_type=F32)
        g = jnp.dot(x, w3_ref[...].astype(BF16), preferred_element_type=F32)

        def finish(at, gt):
            hb[0:m, :] = ((at * jax.nn.sigmoid(at)) * gt).astype(BF16)

        if nkc == 1:
            finish(a, g)
            return

        @pl.when(c == 0)
        def _():
            acc_a[0:m, :] = a
            acc_g[0:m, :] = g

        @pl.when(jnp.logical_and(c > 0, c < nkc - 1))
        def _():
            acc_a[0:m, :] += a
            acc_g[0:m, :] += g

        @pl.when(c == nkc - 1)
        def _():
            finish(acc_a[0:m, :] + a, acc_g[0:m, :] + g)

    def project(m):
        o_ref[0:m, :] = jnp.dot(hb[0:m, :], w2_ref[...].astype(BF16), preferred_element_type=F32)
        if m < rmax:
            o_ref[m:rmax, :] = jnp.zeros((rmax - m, o_ref.shape[1]), F32)

    small = (rb - 1) * bm
    for cond, m in ((jnp.logical_and(nb > 0, nb < rb), small), (nb == rb, rmax)):
        @pl.when(jnp.logical_and(cond, c < nkc))
        def _():
            hidden(m)

        @pl.when(jnp.logical_and(cond, c >= nkc))
        def _():
            project(m)

    @pl.when(jnp.logical_and(nb == 0, c >= nkc))
    def _():
        o_ref[...] = jnp.zeros_like(o_ref)


def _experts(x, sb_e, sb_nb, n_used, slot_tok, w1, w3, w2, bm, rb):
    n, d = x.shape
    e, _, hd = w1.shape
    n_sb = sb_e.shape[0]
    rmax = rb * bm
    tk, oc = min(MOE_KC, d), min(MOE_OC, d)
    nkc, noc = d // tk, d // oc

    def w13_map(i, c, se, nb, nu, tok):
        return se[i], jnp.where(i < nu[0], jnp.minimum(c, nkc - 1), nkc - 1), 0

    def w2_map(i, c, se, nb, nu, tok):
        return se[i], 0, jnp.where(i < nu[0], jnp.clip(c - nkc, 0, noc - 1), noc - 1)

    def out_map(i, c, se, nb, nu, tok):
        writing = c >= nkc
        row = jnp.where(writing, i, jnp.maximum(i - 1, 0))
        col = jnp.where(writing, c - nkc, jnp.where(i == 0, 0, noc - 1))
        return row, col

    grid_spec = pltpu.PrefetchScalarGridSpec(
        num_scalar_prefetch=4,
        grid=(n_sb, nkc + noc),
        in_specs=[pl.BlockSpec(memory_space=pl.ANY),
                  pl.BlockSpec((None, tk, hd), w13_map),
                  pl.BlockSpec((None, tk, hd), w13_map),
                  pl.BlockSpec((None, hd, oc), w2_map)],
        out_specs=pl.BlockSpec((rmax, oc), out_map),
        scratch_shapes=[pltpu.VMEM((rmax, d), F32), pltpu.VMEM((nkc, rmax, tk), BF16),
                        pltpu.VMEM((rmax, hd), F32), pltpu.VMEM((rmax, hd), F32),
                        pltpu.VMEM((rmax, hd), BF16), pltpu.SemaphoreType.DMA(())],
    )
    return pl.pallas_call(
        functools.partial(_expert_body, bm=bm, rb=rb, nkc=nkc),
        grid_spec=grid_spec,
        out_shape=jax.ShapeDtypeStruct((n_sb * rmax, d), F32),
        compiler_params=_cparams("arbitrary", "arbitrary"),
        name="moe_experts",
    )(sb_e, sb_nb, n_used, slot_tok, x, w1, w3, w2)


def _combine_body(slot_ref, y_hbm, w_ref, x_ref, g_ref, b_ref, o_ref, buf, sem, *, tb):
    i = pl.program_id(0)

    def gather(tile, half):
        def issue(r, c):
            for kk in range(MOE_TOPK):
                _row_copy(y_hbm, slot_ref[(tile * tb + r) * MOE_TOPK + kk], buf.at[half, kk], r,
                          sem.at[half]).start()
            return c
        lax.fori_loop(0, tb, issue, 0)

    @pl.when(i == 0)
    def _():
        gather(0, 0)

    @pl.when(i + 1 < pl.num_programs(0))
    def _():
        gather(i + 1, (i + 1) % 2)

    half = i % 2

    def wait(r, c):
        for kk in range(MOE_TOPK):
            _row_copy(y_hbm, 0, buf.at[half, kk], r, sem.at[half]).wait()
        return c
    lax.fori_loop(0, tb, wait, 0)
    w = w_ref[...]
    y = w[:, 0:1] * buf[half, 0] + w[:, 1:2] * buf[half, 1]
    o_ref[...] = _layer_norm(DEEPNORM_ALPHA * x_ref[...] + y, g_ref[...], b_ref[...])


def _combine_ln(slot_of, yb, weights, x, g, b, tb):
    n, d = x.shape
    grid_spec = pltpu.PrefetchScalarGridSpec(
        num_scalar_prefetch=1,
        grid=(n // tb,),
        in_specs=[pl.BlockSpec(memory_space=pl.ANY),
                  pl.BlockSpec((tb, LANES), lambda i, s: (i, 0)),
                  pl.BlockSpec((tb, d), lambda i, s: (i, 0)),
                  pl.BlockSpec((1, d), lambda i, s: (0, 0)),
                  pl.BlockSpec((1, d), lambda i, s: (0, 0))],
        out_specs=pl.BlockSpec((tb, d), lambda i, s: (i, 0)),
        scratch_shapes=[pltpu.VMEM((2, MOE_TOPK, tb, d), F32), pltpu.SemaphoreType.DMA((2,))],
    )
    return pl.pallas_call(
        functools.partial(_combine_body, tb=tb),
        grid_spec=grid_spec,
        out_shape=jax.ShapeDtypeStruct((n, d), F32),
        compiler_params=_cparams("arbitrary"),
        name="moe_combine_ln",
    )(slot_of, yb, weights, x, g.reshape(1, d), b.reshape(1, d))


def _ln_moe_ln(x, mix, ln1_g, ln1_b, router_w1, router_b1, router_w2, router_b2, exp_w1, exp_w3, exp_w2,
               ln_g, ln_b):
    n, d = x.shape
    e = exp_w1.shape[0]
    bm = MOE_BM
    h1, weights, ids, cnt = _ln_router(x, mix, ln1_g, ln1_b, router_w1, router_b1, router_w2, router_b2)
    m = n * MOE_TOPK
    rb = MOE_RB
    rmax = rb * bm
    eid_f = ids[:, :MOE_TOPK].reshape(m)
    rank = ids[:, MOE_TOPK:2 * MOE_TOPK].reshape(m)
    counts = cnt[0, MOE_GROUPS:MOE_GROUPS + e].astype(jnp.int32)
    nblk_e = (counts + bm - 1) // bm
    ngrp_e = (nblk_e + rb - 1) // rb
    gend = jnp.cumsum(ngrp_e)
    gstart = gend - ngrp_e
    slot_of = ((gstart[eid_f] + rank // rmax) * rmax + rank % rmax).astype(jnp.int32)
    n_grp = -(-(-(-m // bm) + e * rb) // rb)
    tok_f = jnp.arange(m, dtype=jnp.int32) // MOE_TOPK
    slot_tok = jnp.zeros((n_grp * rmax,), jnp.int32).at[slot_of].set(tok_f)
    n_used = gend[-1:].astype(jnp.int32)
    gidx = jnp.arange(n_grp, dtype=jnp.int32)
    gcl = jnp.minimum(gidx, n_used[0] - 1)
    grp_e = jnp.minimum(jnp.sum(gend[None, :] <= gcl[:, None], axis=1), e - 1).astype(jnp.int32)
    local = gcl - gstart[grp_e]
    grp_nb = jnp.where(gidx < n_used[0], jnp.minimum(rb, nblk_e[grp_e] - local * rb), 0).astype(jnp.int32)
    yb = _experts(h1, grp_e, grp_nb, n_used, slot_tok, exp_w1, exp_w3, exp_w2, bm, rb)
    return _combine_ln(slot_of, yb, weights, h1, ln_g, ln_b, min(MOE_TB, n))


def kernel(x, w_in, hy_conv_w, hy_conv_b, hy_f_w1, hy_f_b1, hy_f_fr1, hy_f_w2, hy_f_b2, hy_f_fr2,
           hy_f_w3, hy_f_b3, hy_f_fr3, hy_f_wout, hy_bias, ml_gate_bias, ml_norm_g, p_hy, p_ml, w_out,
           ln1_g, ln1_b, router_w1, router_b1, router_w2, router_b2, exp_w1, exp_w3, exp_w2,
           ln2_g, ln2_b):
    B, L, D = x.shape
    N = B * L
    C = D // 2
    H = ML_HEADS
    dv = C // H
    dk = dv // 2
    col_q = (HY_ORDER + 1) * C
    col_if = col_q + 2 * H * dk + 2 * C
    col_gate = col_if + 4 * H

    xf = x.reshape(N, D)
    xb = xf.astype(BF16)
    w_hy = w_in[:, :col_q].astype(BF16)
    w_ml = w_in[:, col_q:col_if].astype(BF16)
    w_if = jnp.zeros((D, LANES), BF16).at[:, :4 * H].set(w_in[:, col_if:col_gate].astype(BF16))
    w_gt = w_in[:, col_gate:].astype(BF16)

    z_hy = _matmul(xb, w_hy, F32, 1024, 512).reshape(B, L, col_q)
    z_ml = _matmul(xb, w_ml, BF16, 1024, 512)
    z_if = _matmul(xb, w_if, F32, 1024, LANES)
    z_gt = _matmul(xb, w_gt, BF16, 1024, 512)

    tables, filter_tables = _dft_tables(L)
    kern = _hyena_filters(L, C, hy_f_w1, hy_f_b1, hy_f_fr1, hy_f_w2, hy_f_b2, hy_f_fr2,
                          hy_f_w3, hy_f_b3, hy_f_fr3, hy_f_wout, hy_bias)
    n2 = 2 * L // FFT_N1
    kfreq = _filter_fft(_row_permute(kern, FFT_N1, n2, F32), *filter_tables)
    u = _short_conv(z_hy, hy_conv_w, hy_conv_b, n2)
    v1 = _long_conv(u, 2, u, 0, kfreq, 0, tables, C)
    y_hy = _long_conv(v1, 0, u, 1, kfreq, 1, tables, C)
    y_hy = _row_permute(y_hy, n2, L // n2, BF16).reshape(N, C)

    g = z_if[:, :4 * H].reshape(B, L, 4, H)
    gates_c = g.transpose(0, 3, 1, 2)
    gates_r = g.transpose(0, 3, 2, 1)
    bias_c = ml_gate_bias.T.reshape(H, 1, 4)
    bias_r = ml_gate_bias.T.reshape(H, 4, 1)
    h_fwd = _mlstm(z_ml, gates_c, gates_r, bias_c, bias_r, B, L, dk, dv, rev=False)
    y_ml = _mlstm(z_ml, gates_c, gates_r, bias_c, bias_r, B, L, dk, dv, rev=True,
                  hprev=h_fwd, norm_g=ml_norm_g.reshape(1, C))

    merged = _merge(y_hy, y_ml, p_hy.astype(BF16), p_ml.astype(BF16), z_gt)
    mix = _matmul(merged, w_out.astype(BF16), F32, 1024, 512)
    out = _ln_moe_ln(xf, mix, ln1_g, ln1_b, router_w1, router_b1, router_w2, router_b2,
                     exp_w1, exp_w3, exp_w2, ln2_g, ln2_b)
    return out.reshape(B, L, D)
```

```python
import functools
import math

import jax
import jax.numpy as jnp
from jax import lax
from jax.experimental import pallas as pl
from jax.experimental.pallas import tpu as pltpu

F32 = jnp.float32
BF16 = jnp.bfloat16
HIGHEST = lax.Precision.HIGHEST

VMEM_LIMIT_BYTES = 56 * 1024 * 1024
LANES = 128

HY_ORDER = 2
HY_SHORT = 3
HY_POS_EMB = 33
HY_DECAY_TARGET = 1e-2
HY_FAST_DECAY = 0.3
HY_SLOW_DECAY = 1.5
HY_MOD_SHIFT = 0.05
ML_HEADS = 8
MOE_GROUPS = 8
MOE_PER_GROUP = 8
MOE_TOPK = 2
DEPTH = 1
DEEPNORM_ALPHA = (2.0 * DEPTH) ** 0.25
LN_EPS = 1e-5

FFT_N1 = 64
HY_CB = 256
FFT_G = 32
FFT_KC = 16
FFT_UNROLL_OUTER = 4
FFT_UNROLL_INNER = 4
ML_CHUNK = 256
ML_CHUNKS_PER_STEP = 4
MOE_BM = 128
MOE_RB = 5
MOE_KC = 1024
MOE_OC = 512
MOE_TB = 256


def _cparams(*sem):
    return pltpu.CompilerParams(dimension_semantics=sem, vmem_limit_bytes=VMEM_LIMIT_BYTES)


def _mm_body(a_ref, b_ref, o_ref):
    o_ref[...] = jnp.dot(a_ref[...], b_ref[...], preferred_element_type=F32).astype(o_ref.dtype)


def _matmul(a, b, out_dtype, tm, tn):
    m, k = a.shape
    _, n = b.shape
    assert m % tm == 0 and n % tn == 0
    return pl.pallas_call(
        _mm_body,
        grid=(m // tm, n // tn),
        in_specs=[pl.BlockSpec((tm, k), lambda i, j: (i, 0)),
                  pl.BlockSpec((k, tn), lambda i, j: (0, j))],
        out_specs=pl.BlockSpec((tm, tn), lambda i, j: (i, j)),
        out_shape=jax.ShapeDtypeStruct((m, n), out_dtype),
        compiler_params=_cparams("parallel", "arbitrary"),
        name="proj_matmul",
    )(a, b)


def _permute_pitch(outer):
    return outer + 8


def _permute_rows(get_rows, tmp_ref, dst_ref, inner, outer):
    pitch = _permute_pitch(outer)
    for b in range(inner):
        tmp_ref[0, b * pitch:b * pitch + outer, :] = get_rows(b * outer, (b + 1) * outer)

    def it(a, c):
        rows = tmp_ref[0, pl.ds(a, inner, stride=pitch), :]
        dst_ref[0, pl.ds(pl.multiple_of(a * inner, inner), inner), :] = rows.astype(dst_ref.dtype)
        return c
    lax.fori_loop(0, outer, it, 0, unroll=4)


def _short_conv_body(z_ref, w_ref, b_ref, o_ref, tmp, *, n2):
    z = z_ref[0]
    L = z.shape[0]
    row = lax.broadcasted_iota(jnp.int32, z.shape, 0)
    prev = jnp.where(row == 0, 0.0, pltpu.roll(z, 1, 0))
    nxt = jnp.where(row == L - 1, 0.0, pltpu.roll(z, L - 1, 0))
    u = b_ref[...] + prev * w_ref[0:1, :] + z * w_ref[1:2, :] + nxt * w_ref[2:3, :]
    _permute_rows(lambda lo, hi: u[lo:hi], tmp, o_ref, L // n2, n2)


def _short_conv(z, w, b, n2, cb=LANES):
    B, L, C = z.shape
    return pl.pallas_call(
        functools.partial(_short_conv_body, n2=n2),
        grid=(B, C // cb),
        in_specs=[pl.BlockSpec((1, L, cb), lambda i, j: (i, 0, j)),
                  pl.BlockSpec((HY_SHORT, cb), lambda i, j: (0, j)),
                  pl.BlockSpec((1, cb), lambda i, j: (0, j))],
        out_specs=pl.BlockSpec((1, L, cb), lambda i, j: (i, 0, j)),
        out_shape=jax.ShapeDtypeStruct((B, L, C), F32),
        scratch_shapes=[pltpu.VMEM((1, (L // n2) * _permute_pitch(n2), cb), F32)],
        compiler_params=_cparams("parallel", "parallel"),
        name="hy_short_conv",
    )(z, w, b.reshape(1, C))


def _row_permute_body(x_ref, o_ref, tmp, *, inner, outer):
    _permute_rows(lambda lo, hi: x_ref[0, lo:hi, :], tmp, o_ref, inner, outer)


def _row_permute(x, inner, outer, out_dtype, cb=LANES):
    A, R, C = x.shape
    assert R == inner * outer
    cb = min(cb, C)
    return pl.pallas_call(
        functools.partial(_row_permute_body, inner=inner, outer=outer),
        grid=(A, C // cb),
        in_specs=[pl.BlockSpec((1, R, cb), lambda i, j: (i, 0, j))],
        out_specs=pl.BlockSpec((1, R, cb), lambda i, j: (i, 0, j)),
        out_shape=jax.ShapeDtypeStruct((A, R, C), out_dtype),
        scratch_shapes=[pltpu.VMEM((1, inner * _permute_pitch(outer), cb), F32)],
        compiler_params=_cparams("parallel", "parallel"),
        name="hy_row_permute",
    )(x)


def _filter_hidden_body(w1_ref, b1_ref, fr1_ref, w2_ref, b2_ref, fr2_ref, w3_ref, b3_ref, fr3_ref,
                        o_ref, *, L, rows):
    i = pl.program_id(0)
    n = i * rows + lax.broadcasted_iota(jnp.int32, (rows, 1), 0)
    pos = jnp.where(n < L, n, 2 * L - n).astype(F32)
    t = pos / (L - 1.0)
    w = (2.0 * math.pi / L) * pos
    lane = lax.broadcasted_iota(jnp.int32, (1, LANES), 1)
    bands = (HY_POS_EMB - 1) // 2
    band = jnp.where(lane <= bands, lane - 1, lane - 1 - bands).astype(F32)
    freq = 1e-4 + band * ((bands - 1 - 1e-4) / (bands - 1))
    ang = w * freq
    feats = jnp.where(lane == 0, t,
                      jnp.where(lane <= bands, jnp.cos(ang),
                                jnp.where(lane <= 2 * bands, -jnp.sin(ang), 0.0)))
    h = jnp.sin(fr1_ref[...] * (jnp.dot(feats, w1_ref[...], precision=HIGHEST,
                                        preferred_element_type=F32) + b1_ref[...]))
    h = jnp.sin(fr2_ref[...] * (jnp.dot(h, w2_ref[...], precision=HIGHEST,
                                        preferred_element_type=F32) + b2_ref[...]))
    h = jnp.sin(fr3_ref[...] * (jnp.dot(h, w3_ref[...], precision=HIGHEST,
                                        preferred_element_type=F32) + b3_ref[...]))
    o_ref[...] = h


def _filter_out_body(h_ref, wout_ref, delta_ref, bias_ref, o_ref, *, L):
    d = pl.program_id(1)
    h = jnp.dot(h_ref[...], wout_ref[0], precision=HIGHEST, preferred_element_type=F32)
    r = lax.broadcasted_iota(jnp.int32, (L, 1), 0)
    pos = jnp.where(d == 0, r, L - r).astype(F32)
    t = pos / (L - 1.0)
    window = jnp.exp(-t * delta_ref[...]) + HY_MOD_SHIFT
    first = r == 0
    tap = jnp.where(jnp.logical_and(d == 1, first), 0.0, h * window)
    o_ref[0] = tap + jnp.where(jnp.logical_and(d == 0, first), bias_ref[...], 0.0)


def _hyena_filters(L, C, f_w1, f_b1, f_fr1, f_w2, f_b2, f_fr2, f_w3, f_b3, f_fr3, f_wout, bias, cb=512):
    fh = f_w2.shape[0]
    rows = 1024 if (2 * L) % 1024 == 0 else 2 * L
    w1p = jnp.zeros((LANES, fh), F32).at[:HY_POS_EMB].set(f_w1)
    vec = lambda a: a.reshape(1, fh)
    full = lambda shape: pl.BlockSpec(shape, lambda i: (0,) * len(shape))
    hid = pl.pallas_call(
        functools.partial(_filter_hidden_body, L=L, rows=rows),
        grid=(2 * L // rows,),
        in_specs=[full((LANES, fh)), full((1, fh)), full((1, fh)),
                  full((fh, fh)), full((1, fh)), full((1, fh)),
                  full((fh, fh)), full((1, fh)), full((1, fh))],
        out_specs=pl.BlockSpec((rows, fh), lambda i: (i, 0)),
        out_shape=jax.ShapeDtypeStruct((2 * L, fh), F32),
        compiler_params=_cparams("parallel"),
        name="hy_filter_hidden",
    )(w1p, vec(f_b1), vec(f_fr1), f_w2, vec(f_b2), vec(f_fr2), f_w3, vec(f_b3), vec(f_fr3))
    wout = f_wout.reshape(fh, 2, HY_ORDER, C).transpose(1, 2, 0, 3).reshape(2 * HY_ORDER, fh, C)
    deltas = jnp.abs(jnp.linspace(math.log(HY_DECAY_TARGET) / HY_SLOW_DECAY,
                                  math.log(HY_DECAY_TARGET) / HY_FAST_DECAY, C, dtype=F32))
    cb = min(cb, C)
    return pl.pallas_call(
        functools.partial(_filter_out_body, L=L),
        grid=(HY_ORDER, 2, C // cb),
        in_specs=[pl.BlockSpec((L, fh), lambda o, d, j: (d, 0)),
                  pl.BlockSpec((1, fh, cb), lambda o, d, j: (d * HY_ORDER + o, 0, j)),
                  pl.BlockSpec((1, cb), lambda o, d, j: (0, j)),
                  pl.BlockSpec((None, 1, cb), lambda o, d, j: (o, 0, j))],
        out_specs=pl.BlockSpec((1, L, cb), lambda o, d, j: (o, d, j)),
        out_shape=jax.ShapeDtypeStruct((HY_ORDER, 2 * L, C), F32),
        compiler_params=_cparams("parallel", "parallel", "parallel"),
        name="hy_filter_out",
    )(hid, wout, deltas.reshape(1, C), bias.reshape(HY_ORDER, 1, C))


def _dft_tables(L):
    N = 2 * L
    N1 = FFT_N1
    N2 = N // N1
    h = N1 // 2
    n2 = jnp.arange(N2, dtype=jnp.int32)[:, None, None]
    k1 = jnp.arange(N1, dtype=jnp.int32)[None, :, None]
    n1 = jnp.arange(N1, dtype=jnp.int32)[None, None, :]
    ph = (k1 * (N2 * n1 + n2)) % N
    ang = ph.astype(F32) * (-2.0 * math.pi / N)
    mr, mi = jnp.cos(ang), jnp.sin(ang)
    mrp, mip = mr[:, :, :h], mi[:, :, :h]
    t1 = jnp.concatenate([jnp.concatenate([mrp, -mip], axis=2),
                          jnp.concatenate([mip, mrp], axis=2)], axis=1)
    mrt, mit = jnp.swapaxes(mrp, 1, 2) / N, jnp.swapaxes(mip, 1, 2) / N
    t1i = jnp.concatenate([jnp.concatenate([mrt, mit], axis=2),
                           jnp.concatenate([-mit, mrt], axis=2)], axis=1)
    t1f = jnp.concatenate([mr, mi], axis=1)
    a = jnp.arange(N2, dtype=jnp.int32)
    ang2 = ((a[:, None] * a[None, :]) % N2).astype(F32) * (-2.0 * math.pi / N2)
    fr, fi = jnp.cos(ang2), jnp.sin(ang2)
    t2 = jnp.concatenate([jnp.concatenate([fr, -fi], axis=1),
                          jnp.concatenate([fi, fr], axis=1)], axis=0)
    t2i = jnp.concatenate([jnp.concatenate([fr, fi], axis=1),
                           jnp.concatenate([-fi, fr], axis=1)], axis=0)
    conv_tables = tuple(t.astype(BF16) for t in (t1, t1i, t2, t2i))
    filter_tables = (_hi_lo_rows(t1f), _hi_lo_rows(t2))
    return conv_tables, filter_tables


def _split_bf16(d):
    hi = d.astype(BF16)
    return hi, (d - hi.astype(F32)).astype(BF16)


def _hi_lo_rows(t):
    hi, lo = _split_bf16(t)
    return jnp.concatenate([hi, lo], axis=-2)


def _ld_lanes(ref, rows):
    return jnp.concatenate([ref[i, rows, :] for i in range(ref.shape[0])], axis=1)


def _st_lanes(ref, rows, val):
    for i in range(ref.shape[0]):
        ref[i, rows, :] = val[:, i * LANES:(i + 1) * LANES]


def _dot3(t, d, m):
    d_hi, d_lo = _split_bf16(d)
    y = jnp.dot(t, d_hi, preferred_element_type=F32)
    return y[:m] + y[m:] + jnp.dot(t[:m], d_lo, preferred_element_type=F32)


def _dot1(t, d):
    return jnp.dot(t, d.astype(BF16), preferred_element_type=F32)


def _work_pitch(N2):
    return N2 + 8


def _fft_steps(N1, N2):
    return min(FFT_G, N2), min(FFT_KC, N1)


def _filter_fft_body(k_ref, t1f_ref, t2_ref, o_ref, sr, si, *, N1, N2, G, KC):
    s = pl.program_id(2)
    sa = N2 // G
    P = _work_pitch(N2)

    @pl.when(s < sa)
    def _():
        def it(r, c):
            n2 = s * G + r
            rows = k_ref[pl.ds(pl.multiple_of(r * N1, N1), N1), :]
            a = _dot3(t1f_ref[n2], rows, 2 * N1)
            _st_lanes(sr, pl.ds(n2, N1, stride=P), a[:N1])
            _st_lanes(si, pl.ds(n2, N1, stride=P), a[N1:])
            return c
        lax.fori_loop(0, G, it, 0, unroll=FFT_UNROLL_OUTER)

    @pl.when(s >= sa)
    def _():
        t2 = t2_ref[...]

        def it(kk, c):
            r0 = pl.multiple_of(((s - sa) * KC + kk) * P, 8)
            q0 = pl.multiple_of(kk * N2, N2)
            x = _dot3(t2, jnp.concatenate([_ld_lanes(sr, pl.ds(r0, N2)), _ld_lanes(si, pl.ds(r0, N2))],
                                          axis=0), 2 * N2)
            o_ref[0, pl.ds(q0, N2), :] = x[:N2]
            o_ref[1, pl.ds(q0, N2), :] = x[N2:]
            return c
        lax.fori_loop(0, KC, it, 0, unroll=FFT_UNROLL_INNER)


def _filter_fft(kern, t1f, t2):
    O, N, C = kern.shape
    N1, N2 = FFT_N1, N // FFT_N1
    G, KC = _fft_steps(N1, N2)
    sa, sb = N2 // G, N1 // KC
    cb = min(HY_CB, C)
    full = lambda shape: pl.BlockSpec(shape, lambda o, j, s: (0,) * len(shape),
                                      pipeline_mode=pl.Buffered(1))
    return pl.pallas_call(
        functools.partial(_filter_fft_body, N1=N1, N2=N2, G=G, KC=KC),
        grid=(O, C // cb, sa + sb),
        in_specs=[pl.BlockSpec((None, G * N1, cb), lambda o, j, s: (o, jnp.minimum(s, sa - 1), j)),
                  full(t1f.shape), full(t2.shape)],
        out_specs=pl.BlockSpec((None, 2, KC * N2, cb),
                               lambda o, j, s: (o, 0, jnp.maximum(s - sa, 0), j)),
        out_shape=jax.ShapeDtypeStruct((O, 2, N, C), F32),
        scratch_shapes=[pltpu.VMEM((cb // LANES, N1 * _work_pitch(N2), LANES), F32)] * 2,
        compiler_params=_cparams("parallel", "parallel", "arbitrary"),
        name="hy_filter_fft",
    )(kern, t1f, t2)


def _long_conv_body(v_ref, g_ref, kf_ref, t1_ref, t1i_ref, t2_ref, t2i_ref, o_ref, sr, si,
                    *, N1, N2, G, KC):
    s = pl.program_id(2)
    sa, sb = N2 // G, N1 // KC
    h = N1 // 2
    P = _work_pitch(N2)

    @pl.when(s < sa)
    def _():
        def it(r, c):
            n2 = s * G + r
            q = pl.ds(pl.multiple_of(r * h, h), h)
            d = jnp.concatenate([v_ref[0, q, :], v_ref[1, q, :]], axis=0)
            a = _dot1(t1_ref[n2], d)
            _st_lanes(sr, pl.ds(n2, N1, stride=P), a[:N1])
            _st_lanes(si, pl.ds(n2, N1, stride=P), a[N1:])
            return c
        lax.fori_loop(0, G, it, 0, unroll=FFT_UNROLL_OUTER)

    @pl.when(jnp.logical_and(s >= sa, s < sa + sb))
    def _():
        t2, t2i = t2_ref[...], t2i_ref[...]

        def it(kk, c):
            r0 = pl.multiple_of(((s - sa) * KC + kk) * P, 8)
            q0 = pl.multiple_of(kk * N2, N2)
            x = _dot1(t2, jnp.concatenate([_ld_lanes(sr, pl.ds(r0, N2)), _ld_lanes(si, pl.ds(r0, N2))],
                                          axis=0))
            xr, xi = x[:N2], x[N2:]
            kr = kf_ref[0, pl.ds(q0, N2), :]
            ki = kf_ref[1, pl.ds(q0, N2), :]
            p = jnp.concatenate([xr * kr - xi * ki, xr * ki + xi * kr], axis=0)
            y = _dot1(t2i, p)
            _st_lanes(sr, pl.ds(r0, N2), y[:N2])
            _st_lanes(si, pl.ds(r0, N2), y[N2:])
            return c
        lax.fori_loop(0, KC, it, 0, unroll=FFT_UNROLL_INNER)

    @pl.when(s >= sa + sb)
    def _():
        def it(r, c):
            n2 = (s - sa - sb) * G + r
            q = pl.ds(pl.multiple_of(r * h, h), h)
            d = jnp.concatenate([_ld_lanes(sr, pl.ds(n2, N1, stride=P)),
                                 _ld_lanes(si, pl.ds(n2, N1, stride=P))], axis=0)
            y = _dot1(t1i_ref[n2], d)
            for b in range(2):
                o_ref[b, q, :] = g_ref[b, q, :] * y[b * h:(b + 1) * h]
            return c
        lax.fori_loop(0, G, it, 0, unroll=FFT_UNROLL_OUTER)


def _long_conv(v_arr, v_off, g_arr, g_off, kfreq, order, tables, C):
    B, L, _ = v_arr.shape
    t1, t1i, t2, t2i = tables
    N = 2 * L
    N1, N2 = FFT_N1, N // FFT_N1
    G, KC = _fft_steps(N1, N2)
    sa, sb = N2 // G, N1 // KC
    h = N1 // 2
    cb = min(HY_CB, C)
    nj = C // cb
    full = lambda shape: pl.BlockSpec(shape, lambda j, p, s: (0,) * len(shape),
                                      pipeline_mode=pl.Buffered(1))
    last = lambda s: jnp.clip(s - sa - sb, 0, sa - 1)
    return pl.pallas_call(
        functools.partial(_long_conv_body, N1=N1, N2=N2, G=G, KC=KC),
        grid=(nj, B // 2, 2 * sa + sb),
        in_specs=[pl.BlockSpec((2, G * h, cb), lambda j, p, s: (p, jnp.minimum(s, sa - 1), v_off * nj + j)),
                  pl.BlockSpec((2, G * h, cb), lambda j, p, s: (p, last(s), g_off * nj + j)),
                  pl.BlockSpec((None, 2, KC * N2, cb),
                               lambda j, p, s: (order, 0, jnp.clip(s - sa, 0, sb - 1), j)),
                  full(t1.shape), full(t1i.shape), full(t2.shape), full(t2i.shape)],
        out_specs=pl.BlockSpec((2, G * h, cb), lambda j, p, s: (p, last(s), j)),
        out_shape=jax.ShapeDtypeStruct((B, L, C), F32),
        scratch_shapes=[pltpu.VMEM((cb // LANES, N1 * _work_pitch(N2), LANES), F32)] * 2,
        compiler_params=_cparams("parallel", "arbitrary", "arbitrary"),
        name="hy_long_conv",
    )(v_arr, g_arr, kfreq, t1, t1i, t2, t2i)


def _log_sigmoid(x):
    return jnp.minimum(x, 0.0) - jnp.log1p(jnp.exp(-jnp.abs(x)))


def _mlstm_body(*refs, rev, T, nsub, scale, final):
    if final:
        (q_ref, k_ref, v_ref, gc_ref, gr_ref, bc_ref, br_ref, hprev_ref, o_ref, ng_ref,
         out_ref, c_s, n_s, m_s) = refs
    else:
        q_ref, k_ref, v_ref, gc_ref, gr_ref, bc_ref, br_ref, out_ref, c_s, n_s, m_s = refs

    @pl.when(pl.program_id(2) == 0)
    def _():
        c_s[...] = jnp.zeros_like(c_s)
        n_s[...] = jnp.zeros_like(n_s)
        m_s[...] = jnp.zeros_like(m_s)

    gi = 2 if rev else 0
    row = lax.broadcasted_iota(jnp.int32, (T, T), 0)
    col = lax.broadcasted_iota(jnp.int32, (T, T), 1)
    valid = (col >= row) if rev else (col <= row)
    valid_t = (row >= col) if rev else (row <= col)

    def chunk(j):
        rows = slice(j * T, (j + 1) * T)
        gc = gc_ref[rows, :] + bc_ref[...]
        gr = gr_ref[:, rows] + br_ref[...]
        li_c, lf_c = gc[:, gi:gi + 1], _log_sigmoid(gc[:, gi + 1:gi + 2])
        li_r, lf_r = gr[gi:gi + 1, :], _log_sigmoid(gr[gi + 1:gi + 2, :])
        b_c = jnp.sum(jnp.where(valid, lf_r, 0.0), axis=1, keepdims=True)
        b_r = jnp.sum(jnp.where(valid_t, lf_c, 0.0), axis=0, keepdims=True)
        m = m_s[...]
        d = jnp.where(valid, b_c - b_r + li_r, -jnp.inf)
        inter = b_c + m
        m_t = jnp.maximum(inter, jnp.max(d, axis=1, keepdims=True))
        q, k, v = q_ref[rows, :], k_ref[rows, :], v_ref[rows, :]
        qk = lax.dot_general(q, k, (((1,), (1,)), ((), ())), preferred_element_type=F32)
        s = qk * scale * jnp.exp(d - m_t)
        w_inter = jnp.exp(inter - m_t)
        qc = jnp.dot(q, c_s[...].astype(BF16), preferred_element_type=F32) * scale
        num = jnp.dot(s.astype(BF16), v, preferred_element_type=F32) + w_inter * qc
        qn = jnp.sum(q.astype(F32) * n_s[...], axis=1, keepdims=True) * scale
        den = jnp.sum(s, axis=1, keepdims=True) + w_inter * qn
        hout = num / jnp.maximum(jnp.abs(den), jnp.exp(-m_t))

        b_last = b_c[0:1, :] if rev else b_c[T - 1:T, :]
        w_c = b_last - b_c + li_c
        m_new = jnp.maximum(b_last + m, jnp.max(w_c, axis=0, keepdims=True))
        kw = k.astype(F32) * jnp.exp(w_c - m_new)
        decay = jnp.exp(b_last + m - m_new)
        c_s[...] = decay * c_s[...] + lax.dot_general(
            kw.astype(BF16), v, (((0,), (0,)), ((), ())), preferred_element_type=F32)
        n_s[...] = decay * n_s[...] + jnp.sum(kw, axis=0, keepdims=True)
        m_s[...] = m_new

        if final:
            hsum = hout + hprev_ref[rows, :]
            mu = jnp.mean(hsum, axis=1, keepdims=True)
            var = jnp.mean(jnp.square(hsum - mu), axis=1, keepdims=True)
            hn = (hsum - mu) * lax.rsqrt(var + LN_EPS) * ng_ref[...]
            out_ref[rows, :] = (jax.nn.sigmoid(o_ref[rows, :].astype(F32)) * hn).astype(out_ref.dtype)
        else:
            out_ref[rows, :] = hout

    for j in (reversed(range(nsub)) if rev else range(nsub)):
        chunk(j)


def _mlstm(zq, gates_c, gates_r, bias_c, bias_r, B, L, dk, dv, rev, hprev=None, norm_g=None):
    H = ML_HEADS
    T = min(ML_CHUNK, L)
    nsub = min(ML_CHUNKS_PER_STEP, L // T)
    tb = T * nsub
    nc = L // tb
    final = hprev is not None
    cidx = (lambda c: nc - 1 - c) if rev else (lambda c: c)
    kq, kv = H * dk // dk, (2 * H * dk) // dv
    in_specs = [
        pl.BlockSpec((tb, dk), lambda b, h, c: (b * nc + cidx(c), h)),
        pl.BlockSpec((tb, dk), lambda b, h, c: (b * nc + cidx(c), kq + h)),
        pl.BlockSpec((tb, dv), lambda b, h, c: (b * nc + cidx(c), kv + h)),
        pl.BlockSpec((None, None, tb, 4), lambda b, h, c: (b, h, cidx(c), 0)),
        pl.BlockSpec((None, None, 4, tb), lambda b, h, c: (b, h, 0, cidx(c))),
        pl.BlockSpec((None, 1, 4), lambda b, h, c: (h, 0, 0)),
        pl.BlockSpec((None, 4, 1), lambda b, h, c: (h, 0, 0)),
    ]
    args = [zq, zq, zq, gates_c, gates_r, bias_c, bias_r]
    if final:
        in_specs += [
            pl.BlockSpec((tb, dv), lambda b, h, c: (b * nc + cidx(c), h)),
            pl.BlockSpec((tb, dv), lambda b, h, c: (b * nc + cidx(c), kv + H + h)),
            pl.BlockSpec((1, dv), lambda b, h, c: (0, h)),
        ]
        args += [hprev, zq, norm_g]
    return pl.pallas_call(
        functools.partial(_mlstm_body, rev=rev, T=T, nsub=nsub, scale=dk ** -0.5, final=final),
        grid=(B, H, nc),
        in_specs=in_specs,
        out_specs=pl.BlockSpec((tb, dv), lambda b, h, c: (b * nc + cidx(c), h)),
        out_shape=jax.ShapeDtypeStruct((B * L, H * dv), BF16 if final else F32),
        scratch_shapes=[pltpu.VMEM((dk, dv), F32), pltpu.VMEM((1, dk), F32), pltpu.VMEM((1, 1), F32)],
        compiler_params=_cparams("parallel", "parallel", "arbitrary"),
        name="mlstm_bwd" if rev else "mlstm_fwd",
    )(*args)


def _merge_body(yh_ref, ym_ref, ph_ref, pm_ref, gh_ref, gm_ref, o_ref):
    a = jnp.dot(yh_ref[...], ph_ref[...], preferred_element_type=F32)
    b = jnp.dot(ym_ref[...], pm_ref[...], preferred_element_type=F32)
    o_ref[...] = (jax.nn.sigmoid(gh_ref[...].astype(F32)) * a
                  + jax.nn.sigmoid(gm_ref[...].astype(F32)) * b).astype(o_ref.dtype)


def _merge(y_hy, y_ml, p_hy, p_ml, gates, tm=1024, tn=512):
    n, kh = y_hy.shape
    km = y_ml.shape[1]
    d = p_hy.shape[1]
    nj = d // tn
    return pl.pallas_call(
        _merge_body,
        grid=(n // tm, nj),
        in_specs=[pl.BlockSpec((tm, kh), lambda i, j: (i, 0)),
                  pl.BlockSpec((tm, km), lambda i, j: (i, 0)),
                  pl.BlockSpec((kh, tn), lambda i, j: (0, j)),
                  pl.BlockSpec((km, tn), lambda i, j: (0, j)),
                  pl.BlockSpec((tm, tn), lambda i, j: (i, j)),
                  pl.BlockSpec((tm, tn), lambda i, j: (i, nj + j))],
        out_specs=pl.BlockSpec((tm, tn), lambda i, j: (i, j)),
        out_shape=jax.ShapeDtypeStruct((n, d), BF16),
        compiler_params=_cparams("parallel", "arbitrary"),
        name="gated_merge",
    )(y_hy, y_ml, p_hy, p_ml, gates, gates)


def _layer_norm(x, g, b):
    mu = jnp.mean(x, axis=-1, keepdims=True)
    var = jnp.mean(jnp.square(x - mu), axis=-1, keepdims=True)
    return (x - mu) * lax.rsqrt(var + LN_EPS) * g + b


def _router_body(x_ref, mix_ref, lg_ref, lb_ref, whi_ref, wlo_ref, b_ref, h_ref, wout_ref, eout_ref,
                 cnt_ref):
    G, PG = MOE_GROUPS, MOE_PER_GROUP

    @pl.when(pl.program_id(0) == 0)
    def _():
        cnt_ref[...] = jnp.zeros_like(cnt_ref)

    h = _layer_norm(DEEPNORM_ALPHA * x_ref[...] + mix_ref[...], lg_ref[...], lb_ref[...])
    h_ref[...] = h
    h_hi, h_lo = _split_bf16(h)
    logits = (jnp.dot(h_hi, whi_ref[...], preferred_element_type=F32)
              + jnp.dot(h_lo, whi_ref[...], preferred_element_type=F32)
              + jnp.dot(h_hi, wlo_ref[...], preferred_element_type=F32)) + b_ref[...]
    lane = lax.broadcasted_iota(jnp.int32, logits.shape, 1)
    ninf = -jnp.inf
    first = lambda mask: jnp.min(jnp.where(mask, lane, 2 * LANES), axis=1, keepdims=True)
    lg1 = jnp.where(lane < G, logits, ninf)
    m1 = jnp.max(lg1, axis=1, keepdims=True)
    g_sel = first(lg1 == m1)
    p_group = 1.0 / jnp.sum(jnp.exp(lg1 - m1), axis=1, keepdims=True)
    lo = G + g_sel * PG
    in_grp = jnp.logical_and(lane >= lo, lane < lo + PG)
    lg2 = jnp.where(in_grp, logits, ninf)
    m2 = jnp.max(lg2, axis=1, keepdims=True)
    e2 = jnp.exp(lg2 - m2)
    p2 = jnp.where(in_grp, e2 / jnp.sum(e2, axis=1, keepdims=True), -1.0)
    t1 = jnp.max(p2, axis=1, keepdims=True)
    j1 = first(p2 == t1)
    p2b = jnp.where(lane == j1, -1.0, p2)
    t2 = jnp.max(p2b, axis=1, keepdims=True)
    j2 = first(p2b == t2)
    tot = t1 + t2
    oh1, oh2 = lane == j1, lane == j2
    ohs = jnp.where(jnp.logical_or(oh1, oh2), 1.0, 0.0)
    tm = ohs.shape[0]
    earlier = (lax.broadcasted_iota(jnp.int32, (tm, tm), 1)
               < lax.broadcasted_iota(jnp.int32, (tm, tm), 0))
    base = cnt_ref[...] + jnp.dot(jnp.where(earlier, 1.0, 0.0).astype(BF16), ohs.astype(BF16),
                                  preferred_element_type=F32)
    r1 = jnp.sum(jnp.where(oh1, base, 0.0), axis=1, keepdims=True).astype(jnp.int32)
    r2 = jnp.sum(jnp.where(oh2, base, 0.0), axis=1, keepdims=True).astype(jnp.int32)
    cnt_ref[...] += jnp.sum(ohs, axis=0, keepdims=True)
    wout_ref[...] = jnp.where(lane == 0, p_group * (t1 / tot),
                              jnp.where(lane == 1, p_group * (t2 / tot), 0.0))
    eout_ref[...] = jnp.where(lane == 0, j1 - G, jnp.where(lane == 1, j2 - G,
                              jnp.where(lane == 2, r1, jnp.where(lane == 3, r2, 0))))


def _ln_router(x, mix, ln_g, ln_b, router_w1, router_b1, router_w2, router_b2, tm=256):
    n, d = x.shape
    ncol = MOE_GROUPS + MOE_GROUPS * MOE_PER_GROUP
    w = jnp.zeros((d, LANES), F32).at[:, :ncol].set(jnp.concatenate([router_w1, router_w2], axis=1))
    b = jnp.zeros((1, LANES), F32).at[0, :ncol].set(jnp.concatenate([router_b1, router_b2]))
    row = lambda width: pl.BlockSpec((tm, width), lambda i: (i, 0))
    const = lambda shape: pl.BlockSpec(shape, lambda i: (0, 0))
    return pl.pallas_call(
        _router_body,
        grid=(n // tm,),
        in_specs=[row(d), row(d), const((1, d)), const((1, d)), const((d, LANES)), const((d, LANES)),
                  const((1, LANES))],
        out_specs=[row(d), row(LANES), row(LANES), const((1, LANES))],
        out_shape=[jax.ShapeDtypeStruct((n, d), F32), jax.ShapeDtypeStruct((n, LANES), F32),
                   jax.ShapeDtypeStruct((n, LANES), jnp.int32), jax.ShapeDtypeStruct((1, LANES), F32)],
        compiler_params=_cparams("arbitrary"),
        name="ln_moe_router",
    )(x, mix, ln_g.reshape(1, d), ln_b.reshape(1, d), *_split_bf16(w), b)


def _row_copy(src_hbm, row, dst_vmem, r, sem):
    return pltpu.make_async_copy(src_hbm.at[pl.ds(row, 1), :], dst_vmem.at[pl.ds(r, 1), :], sem)


def _pack_bf16_pair(lo, hi):
    bits = lambda a: lax.bitcast_convert_type(a.astype(BF16).astype(F32), jnp.uint32)
    return (bits(hi) & jnp.uint32(0xFFFF0000)) | (bits(lo) >> 16)


def _unpack_bf16_pair(words):
    lo = lax.bitcast_convert_type(words << 16, F32)
    hi = lax.bitcast_convert_type(words & jnp.uint32(0xFFFF0000), F32)
    return lo, hi


def _expert_body(e_ref, nb_ref, nused_ref, tok_ref, x_hbm, w1_ref, w3_ref, w2_ref, w2b_ref, o_ref,
                 stage, xb, acc_a, acc_g, hb, sem, *, bm, rb, nkc):
    del e_ref, nused_ref
    i, c = pl.program_id(0), pl.program_id(1)
    nb = nb_ref[i]
    rmax = rb * bm
    tk = xb.shape[2]

    def for_rows(grp, fn):
        def block(b, carry):
            def body(r, carry2):
                row = b * bm + r
                fn(row, tok_ref[grp * rmax + row])
                return carry2
            return lax.fori_loop(0, bm, body, carry, unroll=8)
        lax.fori_loop(0, nb_ref[grp], block, 0)

    @pl.when(jnp.logical_and(i == 0, c == 0))
    def _():
        for_rows(0, lambda r, t: _row_copy(x_hbm, t, stage, r, sem).start())

    @pl.when(jnp.logical_and(c == 0, nb > 0))
    def _():
        for_rows(i, lambda r, t: _row_copy(x_hbm, 0, stage, r, sem).wait())
        for b in range(rb):
            q = pl.ds(b * bm, bm)

            @pl.when(b < nb)
            def _():
                for k in range(nkc):
                    xb[k, q, :] = stage[q, k * tk:(k + 1) * tk].astype(BF16)

            @pl.when(b >= nb)
            def _():
                for k in range(nkc):
                    xb[k, q, :] = jnp.zeros((bm, tk), BF16)

        @pl.when(i + 1 < pl.num_programs(0))
        def _():
            for_rows(i + 1, lambda r, t: _row_copy(x_hbm, t, stage, r, sem).start())

    def hidden(m):
        x = xb[c, 0:m, :]
        a = jnp.dot(x, w1_ref[...].astype(BF16), preferred_element_type=F32)
        g = jnp.dot(x, w3_ref[...].astype(BF16), preferred_element_type=F32)

        def finish(at, gt):
            hb[0:m, :] = ((at * jax.nn.sigmoid(at)) * gt).astype(BF16)

        if nkc == 1:
            finish(a, g)
            return

        @pl.when(c == 0)
        def _():
            acc_a[0:m, :] = a
            acc_g[0:m, :] = g

        @pl.when(jnp.logical_and(c > 0, c < nkc - 1))
        def _():
            acc_a[0:m, :] += a
            acc_g[0:m, :] += g

        @pl.when(c == nkc - 1)
        def _():
            finish(acc_a[0:m, :] + a, acc_g[0:m, :] + g)

    def project(m):
        h = hb[0:m, :]
        lo = jnp.dot(h, w2_ref[...].astype(BF16), preferred_element_type=F32)
        hi = jnp.dot(h, w2b_ref[...].astype(BF16), preferred_element_type=F32)
        o_ref[0:m, :] = _pack_bf16_pair(lo, hi)
        if m < rmax:
            o_ref[m:rmax, :] = jnp.zeros((rmax - m, o_ref.shape[1]), jnp.uint32)

    small = (rb - 1) * bm
    for cond, m in ((jnp.logical_and(nb > 0, nb < rb), small), (nb == rb, rmax)):
        @pl.when(jnp.logical_and(cond, c < nkc))
        def _():
            hidden(m)

        @pl.when(jnp.logical_and(cond, c >= nkc))
        def _():
            project(m)

    @pl.when(jnp.logical_and(nb == 0, c >= nkc))
    def _():
        o_ref[...] = jnp.zeros_like(o_ref)


def _experts(x, sb_e, sb_nb, n_used, slot_tok, w1, w3, w2, bm, rb):
    n, d = x.shape
    e, _, hd = w1.shape
    n_sb = sb_e.shape[0]
    rmax = rb * bm
    dh = d // 2
    tk, oc = min(MOE_KC, d), min(MOE_OC, dh)
    nkc, noc = d // tk, dh // oc

    def w13_map(i, c, se, nb, nu, tok):
        return se[i], jnp.where(i < nu[0], jnp.minimum(c, nkc - 1), nkc - 1), 0

    def w2_chunk(i, c, nu):
        return jnp.where(i < nu[0], jnp.clip(c - nkc, 0, noc - 1), noc - 1)

    def w2_map(i, c, se, nb, nu, tok):
        return se[i], 0, w2_chunk(i, c, nu)

    def w2b_map(i, c, se, nb, nu, tok):
        return se[i], 0, noc + w2_chunk(i, c, nu)

    def out_map(i, c, se, nb, nu, tok):
        writing = c >= nkc
        row = jnp.where(writing, i, jnp.maximum(i - 1, 0))
        col = jnp.where(writing, c - nkc, jnp.where(i == 0, 0, noc - 1))
        return row, col

    grid_spec = pltpu.PrefetchScalarGridSpec(
        num_scalar_prefetch=4,
        grid=(n_sb, nkc + noc),
        in_specs=[pl.BlockSpec(memory_space=pl.ANY),
                  pl.BlockSpec((None, tk, hd), w13_map),
                  pl.BlockSpec((None, tk, hd), w13_map),
                  pl.BlockSpec((None, hd, oc), w2_map),
                  pl.BlockSpec((None, hd, oc), w2b_map)],
        out_specs=pl.BlockSpec((rmax, oc), out_map),
        scratch_shapes=[pltpu.VMEM((rmax, d), F32), pltpu.VMEM((nkc, rmax, tk), BF16),
                        pltpu.VMEM((rmax, hd), F32), pltpu.VMEM((rmax, hd), F32),
                        pltpu.VMEM((rmax, hd), BF16), pltpu.SemaphoreType.DMA(())],
    )
    return pl.pallas_call(
        functools.partial(_expert_body, bm=bm, rb=rb, nkc=nkc),
        grid_spec=grid_spec,
        out_shape=jax.ShapeDtypeStruct((n_sb * rmax, dh), jnp.uint32),
        compiler_params=_cparams("arbitrary", "arbitrary"),
        name="moe_experts",
    )(sb_e, sb_nb, n_used, slot_tok, x, w1, w3, w2, w2)


def _combine_body(slot_ref, y_hbm, w_ref, x_ref, g_ref, b_ref, o_ref, buf, sem, *, tb):
    i = pl.program_id(0)

    def gather(tile, half):
        def issue(r, c):
            for kk in range(MOE_TOPK):
                _row_copy(y_hbm, slot_ref[(tile * tb + r) * MOE_TOPK + kk], buf.at[half, kk], r,
                          sem.at[half]).start()
            return c
        lax.fori_loop(0, tb, issue, 0, unroll=8)

    @pl.when(i == 0)
    def _():
        gather(0, 0)

    @pl.when(i + 1 < pl.num_programs(0))
    def _():
        gather(i + 1, (i + 1) % 2)

    half = i % 2

    def wait(r, c):
        for kk in range(MOE_TOPK):
            _row_copy(y_hbm, 0, buf.at[half, kk], r, sem.at[half]).wait()
        return c
    lax.fori_loop(0, tb, wait, 0, unroll=8)
    w = w_ref[...]
    lo0, hi0 = _unpack_bf16_pair(buf[half, 0])
    lo1, hi1 = _unpack_bf16_pair(buf[half, 1])
    y = jnp.concatenate([w[:, 0:1] * lo0 + w[:, 1:2] * lo1, w[:, 0:1] * hi0 + w[:, 1:2] * hi1], axis=1)
    o_ref[...] = _layer_norm(DEEPNORM_ALPHA * x_ref[...] + y, g_ref[...], b_ref[...])


def _combine_ln(slot_of, yb, weights, x, g, b, tb):
    n, d = x.shape
    grid_spec = pltpu.PrefetchScalarGridSpec(
        num_scalar_prefetch=1,
        grid=(n // tb,),
        in_specs=[pl.BlockSpec(memory_space=pl.ANY),
                  pl.BlockSpec((tb, LANES), lambda i, s: (i, 0)),
                  pl.BlockSpec((tb, d), lambda i, s: (i, 0)),
                  pl.BlockSpec((1, d), lambda i, s: (0, 0)),
                  pl.BlockSpec((1, d), lambda i, s: (0, 0))],
        out_specs=pl.BlockSpec((tb, d), lambda i, s: (i, 0)),
        scratch_shapes=[pltpu.VMEM((2, MOE_TOPK, tb, d // 2), jnp.uint32),
                        pltpu.SemaphoreType.DMA((2,))],
    )
    return pl.pallas_call(
        functools.partial(_combine_body, tb=tb),
        grid_spec=grid_spec,
        out_shape=jax.ShapeDtypeStruct((n, d), F32),
        compiler_params=_cparams("arbitrary"),
        name="moe_combine_ln",
    )(slot_of, yb, weights, x, g.reshape(1, d), b.reshape(1, d))


def _ln_moe_ln(x, mix, ln1_g, ln1_b, router_w1, router_b1, router_w2, router_b2, exp_w1, exp_w3, exp_w2,
               ln_g, ln_b):
    n, d = x.shape
    e = exp_w1.shape[0]
    bm = MOE_BM
    h1, weights, ids, cnt = _ln_router(x, mix, ln1_g, ln1_b, router_w1, router_b1, router_w2, router_b2)
    m = n * MOE_TOPK
    rb = MOE_RB
    rmax = rb * bm
    eid_f = ids[:, :MOE_TOPK].reshape(m)
    rank = ids[:, MOE_TOPK:2 * MOE_TOPK].reshape(m)
    counts = cnt[0, MOE_GROUPS:MOE_GROUPS + e].astype(jnp.int32)
    nblk_e = (counts + bm - 1) // bm
    ngrp_e = (nblk_e + rb - 1) // rb
    gend = jnp.cumsum(ngrp_e)
    gstart = gend - ngrp_e
    slot_of = ((gstart[eid_f] + rank // rmax) * rmax + rank % rmax).astype(jnp.int32)
    n_grp = -(-(-(-m // bm) + e * rb) // rb)
    tok_f = jnp.arange(m, dtype=jnp.int32) // MOE_TOPK
    slot_tok = jnp.zeros((n_grp * rmax,), jnp.int32).at[slot_of].set(tok_f)
    n_used = gend[-1:].astype(jnp.int32)
    gidx = jnp.arange(n_grp, dtype=jnp.int32)
    gcl = jnp.minimum(gidx, n_used[0] - 1)
    grp_e = jnp.minimum(jnp.sum(gend[None, :] <= gcl[:, None], axis=1), e - 1).astype(jnp.int32)
    local = gcl - gstart[grp_e]
    grp_nb = jnp.where(gidx < n_used[0], jnp.minimum(rb, nblk_e[grp_e] - local * rb), 0).astype(jnp.int32)
    yb = _experts(h1, grp_e, grp_nb, n_used, slot_tok, exp_w1, exp_w3, exp_w2, bm, rb)
    return _combine_ln(slot_of, yb, weights, h1, ln_g, ln_b, min(MOE_TB, n))


def kernel(x, w_in, hy_conv_w, hy_conv_b, hy_f_w1, hy_f_b1, hy_f_fr1, hy_f_w2, hy_f_b2, hy_f_fr2,
           hy_f_w3, hy_f_b3, hy_f_fr3, hy_f_wout, hy_bias, ml_gate_bias, ml_norm_g, p_hy, p_ml, w_out,
           ln1_g, ln1_b, router_w1, router_b1, router_w2, router_b2, exp_w1, exp_w3, exp_w2,
           ln2_g, ln2_b):
    B, L, D = x.shape
    N = B * L
    C = D // 2
    H = ML_HEADS
    dv = C // H
    dk = dv // 2
    col_q = (HY_ORDER + 1) * C
    col_if = col_q + 2 * H * dk + 2 * C
    col_gate = col_if + 4 * H

    xf = x.reshape(N, D)
    xb = xf.astype(BF16)
    w_hy = w_in[:, :col_q].astype(BF16)
    w_ml = w_in[:, col_q:col_if].astype(BF16)
    w_if = jnp.zeros((D, LANES), BF16).at[:, :4 * H].set(w_in[:, col_if:col_gate].astype(BF16))
    w_gt = w_in[:, col_gate:].astype(BF16)

    z_hy = _matmul(xb, w_hy, F32, 1024, 512).reshape(B, L, col_q)
    z_ml = _matmul(xb, w_ml, BF16, 1024, 512)
    z_if = _matmul(xb, w_if, F32, 1024, LANES)
    z_gt = _matmul(xb, w_gt, BF16, 1024, 512)

    tables, filter_tables = _dft_tables(L)
    kern = _hyena_filters(L, C, hy_f_w1, hy_f_b1, hy_f_fr1, hy_f_w2, hy_f_b2, hy_f_fr2,
                          hy_f_w3, hy_f_b3, hy_f_fr3, hy_f_wout, hy_bias)
    n2 = 2 * L // FFT_N1
    kfreq = _filter_fft(_row_permute(kern, FFT_N1, n2, F32), *filter_tables)
    u = _short_conv(z_hy, hy_conv_w, hy_conv_b, n2)
    v1 = _long_conv(u, 2, u, 0, kfreq, 0, tables, C)
    y_hy = _long_conv(v1, 0, u, 1, kfreq, 1, tables, C)
    y_hy = _row_permute(y_hy, n2, L // n2, BF16).reshape(N, C)

    g = z_if[:, :4 * H].reshape(B, L, 4, H)
    gates_c = g.transpose(0, 3, 1, 2)
    gates_r = g.transpose(0, 3, 2, 1)
    bias_c = ml_gate_bias.T.reshape(H, 1, 4)
    bias_r = ml_gate_bias.T.reshape(H, 4, 1)
    h_fwd = _mlstm(z_ml, gates_c, gates_r, bias_c, bias_r, B, L, dk, dv, rev=False)
    y_ml = _mlstm(z_ml, gates_c, gates_r, bias_c, bias_r, B, L, dk, dv, rev=True,
                  hprev=h_fwd, norm_g=ml_norm_g.reshape(1, C))

    merged = _merge(y_hy, y_ml, p_hy.astype(BF16), p_ml.astype(BF16), z_gt)
    mix = _matmul(merged, w_out.astype(BF16), F32, 1024, 512)
    out = _ln_moe_ln(xf, mix, ln1_g, ln1_b, router_w1, router_b1, router_w2, router_b2,
                     exp_w1, exp_w3, exp_w2, ln2_g, ln2_b)
    return out.reshape(B, L, D)
```

```python
import functools
import math

import jax
import jax.numpy as jnp
from jax import lax
from jax.experimental import pallas as pl
from jax.experimental.pallas import tpu as pltpu

F32 = jnp.float32
BF16 = jnp.bfloat16
HIGHEST = lax.Precision.HIGHEST

VMEM_LIMIT_BYTES = 56 * 1024 * 1024
LANES = 128

HY_ORDER = 2
HY_SHORT = 3
HY_POS_EMB = 33
HY_DECAY_TARGET = 1e-2
HY_FAST_DECAY = 0.3
HY_SLOW_DECAY = 1.5
HY_MOD_SHIFT = 0.05
ML_HEADS = 8
MOE_GROUPS = 8
MOE_PER_GROUP = 8
MOE_TOPK = 2
DEPTH = 1
DEEPNORM_ALPHA = (2.0 * DEPTH) ** 0.25
LN_EPS = 1e-5

FFT_N1 = 64
HY_CB = 256
FFT_G = 32
FFT_KC = 16
FFT_UNROLL_OUTER = 4
FFT_UNROLL_INNER = 4
ML_CHUNK = 256
ML_CHUNKS_PER_STEP = 4
MOE_BM = 128
MOE_RB = 5
MOE_KC = 1024
MOE_OC = 512
MOE_TB = 256


def _cparams(*sem):
    return pltpu.CompilerParams(dimension_semantics=sem, vmem_limit_bytes=VMEM_LIMIT_BYTES)


def _mm_body(a_ref, b_ref, o_ref):
    o_ref[...] = jnp.dot(a_ref[...], b_ref[...], preferred_element_type=F32).astype(o_ref.dtype)


def _matmul(a, b, out_dtype, tm, tn):
    m, k = a.shape
    _, n = b.shape
    assert m % tm == 0 and n % tn == 0
    return pl.pallas_call(
        _mm_body,
        grid=(m // tm, n // tn),
        in_specs=[pl.BlockSpec((tm, k), lambda i, j: (i, 0)),
                  pl.BlockSpec((k, tn), lambda i, j: (0, j))],
        out_specs=pl.BlockSpec((tm, tn), lambda i, j: (i, j)),
        out_shape=jax.ShapeDtypeStruct((m, n), out_dtype),
        compiler_params=_cparams("parallel", "arbitrary"),
        name="proj_matmul",
    )(a, b)


def _permute_pitch(outer):
    return outer + 8


def _permute_rows(get_rows, tmp_ref, dst_ref, inner, outer):
    pitch = _permute_pitch(outer)
    for b in range(inner):
        tmp_ref[0, b * pitch:b * pitch + outer, :] = get_rows(b * outer, (b + 1) * outer)

    def it(a, c):
        rows = tmp_ref[0, pl.ds(a, inner, stride=pitch), :]
        dst_ref[0, pl.ds(pl.multiple_of(a * inner, inner), inner), :] = rows.astype(dst_ref.dtype)
        return c
    lax.fori_loop(0, outer, it, 0, unroll=4)


def _short_conv_body(z_ref, w_ref, b_ref, o_ref, tmp, *, n2):
    z = z_ref[0]
    L = z.shape[0]
    w0, w1, w2 = w_ref[0:1, :], w_ref[1:2, :], w_ref[2:3, :]
    u = b_ref[...] + pltpu.roll(z, 1, 0) * w0 + z * w1 + pltpu.roll(z, L - 1, 0) * w2
    r8 = lax.broadcasted_iota(jnp.int32, (8, 1), 0)
    head = u[0:8] - jnp.where(r8 == 0, z[L - 1:L] * w0, 0.0)
    tail = u[L - 8:L] - jnp.where(r8 == 7, z[0:1] * w2, 0.0)

    def rows(lo, hi):
        if lo == 0:
            return jnp.concatenate([head, u[8:hi]], axis=0)
        if hi == L:
            return jnp.concatenate([u[lo:L - 8], tail], axis=0)
        return u[lo:hi]
    _permute_rows(rows, tmp, o_ref, L // n2, n2)


def _short_conv(z, w, b, n2, cb=LANES):
    B, L, C = z.shape
    return pl.pallas_call(
        functools.partial(_short_conv_body, n2=n2),
        grid=(B, C // cb),
        in_specs=[pl.BlockSpec((1, L, cb), lambda i, j: (i, 0, j)),
                  pl.BlockSpec((HY_SHORT, cb), lambda i, j: (0, j)),
                  pl.BlockSpec((1, cb), lambda i, j: (0, j))],
        out_specs=pl.BlockSpec((1, L, cb), lambda i, j: (i, 0, j)),
        out_shape=jax.ShapeDtypeStruct((B, L, C), F32),
        scratch_shapes=[pltpu.VMEM((1, (L // n2) * _permute_pitch(n2), cb), F32)],
        compiler_params=_cparams("parallel", "parallel"),
        name="hy_short_conv",
    )(z, w, b.reshape(1, C))


def _row_permute_body(x_ref, o_ref, tmp, *, inner, outer):
    _permute_rows(lambda lo, hi: x_ref[0, lo:hi, :], tmp, o_ref, inner, outer)


def _row_permute(x, inner, outer, out_dtype, cb=LANES):
    A, R, C = x.shape
    assert R == inner * outer
    cb = min(cb, C)
    return pl.pallas_call(
        functools.partial(_row_permute_body, inner=inner, outer=outer),
        grid=(A, C // cb),
        in_specs=[pl.BlockSpec((1, R, cb), lambda i, j: (i, 0, j))],
        out_specs=pl.BlockSpec((1, R, cb), lambda i, j: (i, 0, j)),
        out_shape=jax.ShapeDtypeStruct((A, R, C), out_dtype),
        scratch_shapes=[pltpu.VMEM((1, inner * _permute_pitch(outer), cb), F32)],
        compiler_params=_cparams("parallel", "parallel"),
        name="hy_row_permute",
    )(x)


def _filter_hidden_body(w1_ref, b1_ref, fr1_ref, w2_ref, b2_ref, fr2_ref, w3_ref, b3_ref, fr3_ref,
                        o_ref, *, L, rows):
    i = pl.program_id(0)
    n = i * rows + lax.broadcasted_iota(jnp.int32, (rows, 1), 0)
    pos = jnp.where(n < L, n, 2 * L - n).astype(F32)
    t = pos / (L - 1.0)
    w = (2.0 * math.pi / L) * pos
    lane = lax.broadcasted_iota(jnp.int32, (1, LANES), 1)
    bands = (HY_POS_EMB - 1) // 2
    band = jnp.where(lane <= bands, lane - 1, lane - 1 - bands).astype(F32)
    freq = 1e-4 + band * ((bands - 1 - 1e-4) / (bands - 1))
    ang = w * freq
    feats = jnp.where(lane == 0, t,
                      jnp.where(lane <= bands, jnp.cos(ang),
                                jnp.where(lane <= 2 * bands, -jnp.sin(ang), 0.0)))
    h = jnp.sin(fr1_ref[...] * (jnp.dot(feats, w1_ref[...], precision=HIGHEST,
                                        preferred_element_type=F32) + b1_ref[...]))
    h = jnp.sin(fr2_ref[...] * (jnp.dot(h, w2_ref[...], precision=HIGHEST,
                                        preferred_element_type=F32) + b2_ref[...]))
    h = jnp.sin(fr3_ref[...] * (jnp.dot(h, w3_ref[...], precision=HIGHEST,
                                        preferred_element_type=F32) + b3_ref[...]))
    o_ref[...] = h


def _filter_out_body(h_ref, wout_ref, delta_ref, bias_ref, o_ref, *, L):
    d = pl.program_id(1)
    h = _dot3_split(h_ref[...], wout_ref[0])
    r = lax.broadcasted_iota(jnp.int32, (L, 1), 0)
    pos = jnp.where(d == 0, r, L - r).astype(F32)
    t = pos / (L - 1.0)
    window = jnp.exp(-t * delta_ref[...]) + HY_MOD_SHIFT
    first = r == 0
    tap = jnp.where(jnp.logical_and(d == 1, first), 0.0, h * window)
    o_ref[0] = tap + jnp.where(jnp.logical_and(d == 0, first), bias_ref[...], 0.0)


def _hyena_filters(L, C, f_w1, f_b1, f_fr1, f_w2, f_b2, f_fr2, f_w3, f_b3, f_fr3, f_wout, bias, cb=512):
    fh = f_w2.shape[0]
    rows = 1024 if (2 * L) % 1024 == 0 else 2 * L
    w1p = jnp.zeros((LANES, fh), F32).at[:HY_POS_EMB].set(f_w1)
    vec = lambda a: a.reshape(1, fh)
    full = lambda shape: pl.BlockSpec(shape, lambda i: (0,) * len(shape))
    hid = pl.pallas_call(
        functools.partial(_filter_hidden_body, L=L, rows=rows),
        grid=(2 * L // rows,),
        in_specs=[full((LANES, fh)), full((1, fh)), full((1, fh)),
                  full((fh, fh)), full((1, fh)), full((1, fh)),
                  full((fh, fh)), full((1, fh)), full((1, fh))],
        out_specs=pl.BlockSpec((rows, fh), lambda i: (i, 0)),
        out_shape=jax.ShapeDtypeStruct((2 * L, fh), F32),
        compiler_params=_cparams("parallel"),
        name="hy_filter_hidden",
    )(w1p, vec(f_b1), vec(f_fr1), f_w2, vec(f_b2), vec(f_fr2), f_w3, vec(f_b3), vec(f_fr3))
    wout = f_wout.reshape(fh, 2, HY_ORDER, C).transpose(1, 2, 0, 3).reshape(2 * HY_ORDER, fh, C)
    deltas = jnp.abs(jnp.linspace(math.log(HY_DECAY_TARGET) / HY_SLOW_DECAY,
                                  math.log(HY_DECAY_TARGET) / HY_FAST_DECAY, C, dtype=F32))
    cb = min(cb, C)
    return pl.pallas_call(
        functools.partial(_filter_out_body, L=L),
        grid=(HY_ORDER, 2, C // cb),
        in_specs=[pl.BlockSpec((L, fh), lambda o, d, j: (d, 0)),
                  pl.BlockSpec((1, fh, cb), lambda o, d, j: (d * HY_ORDER + o, 0, j)),
                  pl.BlockSpec((1, cb), lambda o, d, j: (0, j)),
                  pl.BlockSpec((None, 1, cb), lambda o, d, j: (o, 0, j))],
        out_specs=pl.BlockSpec((1, L, cb), lambda o, d, j: (o, d, j)),
        out_shape=jax.ShapeDtypeStruct((HY_ORDER, 2 * L, C), F32),
        compiler_params=_cparams("parallel", "parallel", "parallel"),
        name="hy_filter_out",
    )(hid, wout, deltas.reshape(1, C), bias.reshape(HY_ORDER, 1, C))


def _dft_tables(L):
    N = 2 * L
    N1 = FFT_N1
    N2 = N // N1
    h = N1 // 2
    n2 = jnp.arange(N2, dtype=jnp.int32)[:, None, None]
    k1 = jnp.arange(N1, dtype=jnp.int32)[None, :, None]
    n1 = jnp.arange(N1, dtype=jnp.int32)[None, None, :]
    ph = (k1 * (N2 * n1 + n2)) % N
    ang = ph.astype(F32) * (-2.0 * math.pi / N)
    mr, mi = jnp.cos(ang), jnp.sin(ang)
    mrp, mip = mr[:, :, :h], mi[:, :, :h]
    t1 = jnp.concatenate([jnp.concatenate([mrp, -mip], axis=2),
                          jnp.concatenate([mip, mrp], axis=2)], axis=1)
    mrt, mit = jnp.swapaxes(mrp, 1, 2) / N, jnp.swapaxes(mip, 1, 2) / N
    t1i = jnp.concatenate([jnp.concatenate([mrt, mit], axis=2),
                           jnp.concatenate([-mit, mrt], axis=2)], axis=1)
    t1f = jnp.concatenate([mr, mi], axis=1)
    a = jnp.arange(N2, dtype=jnp.int32)
    ang2 = ((a[:, None] * a[None, :]) % N2).astype(F32) * (-2.0 * math.pi / N2)
    fr, fi = jnp.cos(ang2), jnp.sin(ang2)
    t2 = jnp.concatenate([jnp.concatenate([fr, -fi], axis=1),
                          jnp.concatenate([fi, fr], axis=1)], axis=0)
    t2i = jnp.concatenate([jnp.concatenate([fr, fi], axis=1),
                           jnp.concatenate([-fi, fr], axis=1)], axis=0)
    conv_tables = tuple(t.astype(BF16) for t in (t1, t1i, t2, t2i))
    filter_tables = (_hi_lo_rows(t1f), _hi_lo_rows(t2))
    return conv_tables, filter_tables


def _split_bf16(d):
    hi = d.astype(BF16)
    return hi, (d - hi.astype(F32)).astype(BF16)


def _dot3_split(a, b):
    a_hi, a_lo = _split_bf16(a)
    b_hi, b_lo = _split_bf16(b)
    return (jnp.dot(a_hi, b_hi, preferred_element_type=F32)
            + jnp.dot(a_lo, b_hi, preferred_element_type=F32)
            + jnp.dot(a_hi, b_lo, preferred_element_type=F32))


def _hi_lo_rows(t):
    hi, lo = _split_bf16(t)
    return jnp.concatenate([hi, lo], axis=-2)


def _ld_lanes(ref, rows):
    return jnp.concatenate([ref[i, rows, :] for i in range(ref.shape[0])], axis=1)


def _st_lanes(ref, rows, val):
    for i in range(ref.shape[0]):
        ref[i, rows, :] = val[:, i * LANES:(i + 1) * LANES]


def _dot3(t, d, m):
    d_hi, d_lo = _split_bf16(d)
    y = jnp.dot(t, d_hi, preferred_element_type=F32)
    return y[:m] + y[m:] + jnp.dot(t[:m], d_lo, preferred_element_type=F32)


def _dot1(t, d):
    return jnp.dot(t, d.astype(BF16), preferred_element_type=F32)


def _work_pitch(N2):
    return N2 + 8


def _fft_steps(N1, N2):
    return min(FFT_G, N2), min(FFT_KC, N1)


def _filter_fft_body(k_ref, t1f_ref, t2_ref, o_ref, sr, si, *, N1, N2, G, KC):
    s = pl.program_id(2)
    sa = N2 // G
    P = _work_pitch(N2)

    @pl.when(s < sa)
    def _():
        def it(r, c):
            n2 = s * G + r
            rows = k_ref[pl.ds(pl.multiple_of(r * N1, N1), N1), :]
            a = _dot3(t1f_ref[n2], rows, 2 * N1)
            _st_lanes(sr, pl.ds(n2, N1, stride=P), a[:N1])
            _st_lanes(si, pl.ds(n2, N1, stride=P), a[N1:])
            return c
        lax.fori_loop(0, G, it, 0, unroll=FFT_UNROLL_OUTER)

    @pl.when(s >= sa)
    def _():
        t2 = t2_ref[...]

        def it(kk, c):
            r0 = pl.multiple_of(((s - sa) * KC + kk) * P, 8)
            q0 = pl.multiple_of(kk * N2, N2)
            x = _dot3(t2, jnp.concatenate([_ld_lanes(sr, pl.ds(r0, N2)), _ld_lanes(si, pl.ds(r0, N2))],
                                          axis=0), 2 * N2)
            o_ref[0, pl.ds(q0, N2), :] = x[:N2]
            o_ref[1, pl.ds(q0, N2), :] = x[N2:]
            return c
        lax.fori_loop(0, KC, it, 0, unroll=FFT_UNROLL_INNER)


def _filter_fft(kern, t1f, t2):
    O, N, C = kern.shape
    N1, N2 = FFT_N1, N // FFT_N1
    G, KC = _fft_steps(N1, N2)
    sa, sb = N2 // G, N1 // KC
    cb = min(HY_CB, C)
    full = lambda shape: pl.BlockSpec(shape, lambda o, j, s: (0,) * len(shape),
                                      pipeline_mode=pl.Buffered(1))
    return pl.pallas_call(
        functools.partial(_filter_fft_body, N1=N1, N2=N2, G=G, KC=KC),
        grid=(O, C // cb, sa + sb),
        in_specs=[pl.BlockSpec((None, G * N1, cb), lambda o, j, s: (o, jnp.minimum(s, sa - 1), j)),
                  full(t1f.shape), full(t2.shape)],
        out_specs=pl.BlockSpec((None, 2, KC * N2, cb),
                               lambda o, j, s: (o, 0, jnp.maximum(s - sa, 0), j)),
        out_shape=jax.ShapeDtypeStruct((O, 2, N, C), F32),
        scratch_shapes=[pltpu.VMEM((cb // LANES, N1 * _work_pitch(N2), LANES), F32)] * 2,
        compiler_params=_cparams("parallel", "parallel", "arbitrary"),
        name="hy_filter_fft",
    )(kern, t1f, t2)


def _long_conv_body(v_ref, g_ref, kf_ref, t1_ref, t1i_ref, t2_ref, t2i_ref, o_ref, sr, si,
                    *, N1, N2, G, KC):
    s = pl.program_id(2)
    sa, sb = N2 // G, N1 // KC
    h = N1 // 2
    P = _work_pitch(N2)

    @pl.when(s < sa)
    def _():
        def it(r, c):
            n2 = s * G + r
            q = pl.ds(pl.multiple_of(r * h, h), h)
            d = jnp.concatenate([v_ref[0, q, :], v_ref[1, q, :]], axis=0)
            a = _dot1(t1_ref[n2], d)
            _st_lanes(sr, pl.ds(n2, N1, stride=P), a[:N1])
            _st_lanes(si, pl.ds(n2, N1, stride=P), a[N1:])
            return c
        lax.fori_loop(0, G, it, 0, unroll=FFT_UNROLL_OUTER)

    @pl.when(jnp.logical_and(s >= sa, s < sa + sb))
    def _():
        t2, t2i = t2_ref[...], t2i_ref[...]

        def it(kk, c):
            r0 = pl.multiple_of(((s - sa) * KC + kk) * P, 8)
            q0 = pl.multiple_of(kk * N2, N2)
            x = _dot1(t2, jnp.concatenate([_ld_lanes(sr, pl.ds(r0, N2)), _ld_lanes(si, pl.ds(r0, N2))],
                                          axis=0))
            xr, xi = x[:N2], x[N2:]
            kr = kf_ref[0, pl.ds(q0, N2), :]
            ki = kf_ref[1, pl.ds(q0, N2), :]
            p = jnp.concatenate([xr * kr - xi * ki, xr * ki + xi * kr], axis=0)
            y = _dot1(t2i, p)
            _st_lanes(sr, pl.ds(r0, N2), y[:N2])
            _st_lanes(si, pl.ds(r0, N2), y[N2:])
            return c
        lax.fori_loop(0, KC, it, 0, unroll=FFT_UNROLL_INNER)

    @pl.when(s >= sa + sb)
    def _():
        def it(r, c):
            n2 = (s - sa - sb) * G + r
            q = pl.ds(pl.multiple_of(r * h, h), h)
            d = jnp.concatenate([_ld_lanes(sr, pl.ds(n2, N1, stride=P)),
                                 _ld_lanes(si, pl.ds(n2, N1, stride=P))], axis=0)
            y = _dot1(t1i_ref[n2], d)
            for b in range(2):
                o_ref[b, q, :] = g_ref[b, q, :] * y[b * h:(b + 1) * h]
            return c
        lax.fori_loop(0, G, it, 0, unroll=FFT_UNROLL_OUTER)


def _long_conv(v_arr, v_off, g_arr, g_off, kfreq, order, tables, C):
    B, L, _ = v_arr.shape
    t1, t1i, t2, t2i = tables
    N = 2 * L
    N1, N2 = FFT_N1, N // FFT_N1
    G, KC = _fft_steps(N1, N2)
    sa, sb = N2 // G, N1 // KC
    h = N1 // 2
    cb = min(HY_CB, C)
    nj = C // cb
    full = lambda shape: pl.BlockSpec(shape, lambda j, p, s: (0,) * len(shape),
                                      pipeline_mode=pl.Buffered(1))
    last = lambda s: jnp.clip(s - sa - sb, 0, sa - 1)
    return pl.pallas_call(
        functools.partial(_long_conv_body, N1=N1, N2=N2, G=G, KC=KC),
        grid=(nj, B // 2, 2 * sa + sb),
        in_specs=[pl.BlockSpec((2, G * h, cb), lambda j, p, s: (p, jnp.minimum(s, sa - 1), v_off * nj + j)),
                  pl.BlockSpec((2, G * h, cb), lambda j, p, s: (p, last(s), g_off * nj + j)),
                  pl.BlockSpec((None, 2, KC * N2, cb),
                               lambda j, p, s: (order, 0, jnp.clip(s - sa, 0, sb - 1), j)),
                  full(t1.shape), full(t1i.shape), full(t2.shape), full(t2i.shape)],
        out_specs=pl.BlockSpec((2, G * h, cb), lambda j, p, s: (p, last(s), j)),
        out_shape=jax.ShapeDtypeStruct((B, L, C), F32),
        scratch_shapes=[pltpu.VMEM((cb // LANES, N1 * _work_pitch(N2), LANES), F32)] * 2,
        compiler_params=_cparams("parallel", "arbitrary", "arbitrary"),
        name="hy_long_conv",
    )(v_arr, g_arr, kfreq, t1, t1i, t2, t2i)


def _log_sigmoid(x):
    return jnp.minimum(x, 0.0) - jnp.log1p(jnp.exp(-jnp.abs(x)))


def _mlstm_body(*refs, rev, T, nsub, scale, final):
    if final:
        (q_ref, k_ref, v_ref, gc_ref, gr_ref, bc_ref, br_ref, hprev_ref, o_ref, ng_ref,
         out_ref, c_s, n_s, m_s) = refs
    else:
        q_ref, k_ref, v_ref, gc_ref, gr_ref, bc_ref, br_ref, out_ref, c_s, n_s, m_s = refs

    @pl.when(pl.program_id(2) == 0)
    def _():
        c_s[...] = jnp.zeros_like(c_s)
        n_s[...] = jnp.zeros_like(n_s)
        m_s[...] = jnp.zeros_like(m_s)

    gi = 2 if rev else 0
    row = lax.broadcasted_iota(jnp.int32, (T, T), 0)
    col = lax.broadcasted_iota(jnp.int32, (T, T), 1)
    valid = (col >= row) if rev else (col <= row)
    valid_t = (row >= col) if rev else (row <= col)

    def chunk(j):
        rows = slice(j * T, (j + 1) * T)
        gc = gc_ref[rows, :] + bc_ref[...]
        gr = gr_ref[:, rows] + br_ref[...]
        li_c, lf_c = gc[:, gi:gi + 1], _log_sigmoid(gc[:, gi + 1:gi + 2])
        li_r, lf_r = gr[gi:gi + 1, :], _log_sigmoid(gr[gi + 1:gi + 2, :])
        b_c = jnp.sum(jnp.where(valid, lf_r, 0.0), axis=1, keepdims=True)
        b_r = jnp.sum(jnp.where(valid_t, lf_c, 0.0), axis=0, keepdims=True)
        m = m_s[...]
        d = jnp.where(valid, b_c - b_r + li_r, -jnp.inf)
        inter = b_c + m
        m_t = jnp.maximum(inter, jnp.max(d, axis=1, keepdims=True))
        q, k, v = q_ref[rows, :], k_ref[rows, :], v_ref[rows, :]
        qk = lax.dot_general(q, k, (((1,), (1,)), ((), ())), preferred_element_type=F32)
        s = qk * scale * jnp.exp(d - m_t)
        w_inter = jnp.exp(inter - m_t)
        qc = jnp.dot(q, c_s[...].astype(BF16), preferred_element_type=F32) * scale
        num = jnp.dot(s.astype(BF16), v, preferred_element_type=F32) + w_inter * qc
        qn = jnp.sum(q.astype(F32) * n_s[...], axis=1, keepdims=True) * scale
        den = jnp.sum(s, axis=1, keepdims=True) + w_inter * qn
        hout = num / jnp.maximum(jnp.abs(den), jnp.exp(-m_t))

        b_last = b_c[0:1, :] if rev else b_c[T - 1:T, :]
        w_c = b_last - b_c + li_c
        m_new = jnp.maximum(b_last + m, jnp.max(w_c, axis=0, keepdims=True))
        kw = k.astype(F32) * jnp.exp(w_c - m_new)
        decay = jnp.exp(b_last + m - m_new)
        c_s[...] = decay * c_s[...] + lax.dot_general(
            kw.astype(BF16), v, (((0,), (0,)), ((), ())), preferred_element_type=F32)
        n_s[...] = decay * n_s[...] + jnp.sum(kw, axis=0, keepdims=True)
        m_s[...] = m_new

        if final:
            hsum = hout + hprev_ref[rows, :]
            mu = jnp.mean(hsum, axis=1, keepdims=True)
            var = jnp.mean(jnp.square(hsum - mu), axis=1, keepdims=True)
            hn = (hsum - mu) * lax.rsqrt(var + LN_EPS) * ng_ref[...]
            out_ref[rows, :] = (jax.nn.sigmoid(o_ref[rows, :].astype(F32)) * hn).astype(out_ref.dtype)
        else:
            out_ref[rows, :] = hout

    for j in (reversed(range(nsub)) if rev else range(nsub)):
        chunk(j)


def _mlstm(zq, gates_c, gates_r, bias_c, bias_r, B, L, dk, dv, rev, hprev=None, norm_g=None):
    H = ML_HEADS
    T = min(ML_CHUNK, L)
    nsub = min(ML_CHUNKS_PER_STEP, L // T)
    tb = T * nsub
    nc = L // tb
    final = hprev is not None
    cidx = (lambda c: nc - 1 - c) if rev else (lambda c: c)
    kq, kv = H * dk // dk, (2 * H * dk) // dv
    in_specs = [
        pl.BlockSpec((tb, dk), lambda b, h, c: (b * nc + cidx(c), h)),
        pl.BlockSpec((tb, dk), lambda b, h, c: (b * nc + cidx(c), kq + h)),
        pl.BlockSpec((tb, dv), lambda b, h, c: (b * nc + cidx(c), kv + h)),
        pl.BlockSpec((None, None, tb, 4), lambda b, h, c: (b, h, cidx(c), 0)),
        pl.BlockSpec((None, None, 4, tb), lambda b, h, c: (b, h, 0, cidx(c))),
        pl.BlockSpec((None, 1, 4), lambda b, h, c: (h, 0, 0)),
        pl.BlockSpec((None, 4, 1), lambda b, h, c: (h, 0, 0)),
    ]
    args = [zq, zq, zq, gates_c, gates_r, bias_c, bias_r]
    if final:
        in_specs += [
            pl.BlockSpec((tb, dv), lambda b, h, c: (b * nc + cidx(c), h)),
            pl.BlockSpec((tb, dv), lambda b, h, c: (b * nc + cidx(c), kv + H + h)),
            pl.BlockSpec((1, dv), lambda b, h, c: (0, h)),
        ]
        args += [hprev, zq, norm_g]
    return pl.pallas_call(
        functools.partial(_mlstm_body, rev=rev, T=T, nsub=nsub, scale=dk ** -0.5, final=final),
        grid=(B, H, nc),
        in_specs=in_specs,
        out_specs=pl.BlockSpec((tb, dv), lambda b, h, c: (b * nc + cidx(c), h)),
        out_shape=jax.ShapeDtypeStruct((B * L, H * dv), BF16 if final else F32),
        scratch_shapes=[pltpu.VMEM((dk, dv), F32), pltpu.VMEM((1, dk), F32), pltpu.VMEM((1, 1), F32)],
        compiler_params=_cparams("parallel", "parallel", "arbitrary"),
        name="mlstm_bwd" if rev else "mlstm_fwd",
    )(*args)


def _merge_body(yh_ref, ym_ref, ph_ref, pm_ref, gh_ref, gm_ref, o_ref):
    a = jnp.dot(yh_ref[...], ph_ref[...], preferred_element_type=F32)
    b = jnp.dot(ym_ref[...], pm_ref[...], preferred_element_type=F32)
    o_ref[...] = (jax.nn.sigmoid(gh_ref[...].astype(F32)) * a
                  + jax.nn.sigmoid(gm_ref[...].astype(F32)) * b).astype(o_ref.dtype)


def _merge(y_hy, y_ml, p_hy, p_ml, gates, tm=1024, tn=512):
    n, kh = y_hy.shape
    km = y_ml.shape[1]
    d = p_hy.shape[1]
    nj = d // tn
    return pl.pallas_call(
        _merge_body,
        grid=(n // tm, nj),
        in_specs=[pl.BlockSpec((tm, kh), lambda i, j: (i, 0)),
                  pl.BlockSpec((tm, km), lambda i, j: (i, 0)),
                  pl.BlockSpec((kh, tn), lambda i, j: (0, j)),
                  pl.BlockSpec((km, tn), lambda i, j: (0, j)),
                  pl.BlockSpec((tm, tn), lambda i, j: (i, j)),
                  pl.BlockSpec((tm, tn), lambda i, j: (i, nj + j))],
        out_specs=pl.BlockSpec((tm, tn), lambda i, j: (i, j)),
        out_shape=jax.ShapeDtypeStruct((n, d), BF16),
        compiler_params=_cparams("parallel", "arbitrary"),
        name="gated_merge",
    )(y_hy, y_ml, p_hy, p_ml, gates, gates)


def _layer_norm(x, g, b):
    mu = jnp.mean(x, axis=-1, keepdims=True)
    var = jnp.mean(jnp.square(x - mu), axis=-1, keepdims=True)
    return (x - mu) * lax.rsqrt(var + LN_EPS) * g + b


def _router_body(x_ref, mix_ref, lg_ref, lb_ref, whi_ref, wlo_ref, b_ref, h_ref, wout_ref, eout_ref,
                 cnt_ref):
    G, PG = MOE_GROUPS, MOE_PER_GROUP

    @pl.when(pl.program_id(0) == 0)
    def _():
        cnt_ref[...] = jnp.zeros_like(cnt_ref)

    h = _layer_norm(DEEPNORM_ALPHA * x_ref[...] + mix_ref[...].astype(F32), lg_ref[...], lb_ref[...])
    h_ref[...] = h
    h_hi, h_lo = _split_bf16(h)
    logits = (jnp.dot(h_hi, whi_ref[...], preferred_element_type=F32)
              + jnp.dot(h_lo, whi_ref[...], preferred_element_type=F32)
              + jnp.dot(h_hi, wlo_ref[...], preferred_element_type=F32)) + b_ref[...]
    lane = lax.broadcasted_iota(jnp.int32, logits.shape, 1)
    ninf = -jnp.inf
    first = lambda mask: jnp.min(jnp.where(mask, lane, 2 * LANES), axis=1, keepdims=True)
    lg1 = jnp.where(lane < G, logits, ninf)
    m1 = jnp.max(lg1, axis=1, keepdims=True)
    g_sel = first(lg1 == m1)
    p_group = 1.0 / jnp.sum(jnp.exp(lg1 - m1), axis=1, keepdims=True)
    lo = G + g_sel * PG
    in_grp = jnp.logical_and(lane >= lo, lane < lo + PG)
    lg2 = jnp.where(in_grp, logits, ninf)
    m2 = jnp.max(lg2, axis=1, keepdims=True)
    e2 = jnp.exp(lg2 - m2)
    p2 = jnp.where(in_grp, e2 / jnp.sum(e2, axis=1, keepdims=True), -1.0)
    t1 = jnp.max(p2, axis=1, keepdims=True)
    j1 = first(p2 == t1)
    p2b = jnp.where(lane == j1, -1.0, p2)
    t2 = jnp.max(p2b, axis=1, keepdims=True)
    j2 = first(p2b == t2)
    tot = t1 + t2
    oh1, oh2 = lane == j1, lane == j2
    ohs = jnp.where(jnp.logical_or(oh1, oh2), 1.0, 0.0)
    tm = ohs.shape[0]
    earlier = (lax.broadcasted_iota(jnp.int32, (tm, tm), 1)
               < lax.broadcasted_iota(jnp.int32, (tm, tm), 0))
    base = cnt_ref[...] + jnp.dot(jnp.where(earlier, 1.0, 0.0).astype(BF16), ohs.astype(BF16),
                                  preferred_element_type=F32)
    r1 = jnp.sum(jnp.where(oh1, base, 0.0), axis=1, keepdims=True).astype(jnp.int32)
    r2 = jnp.sum(jnp.where(oh2, base, 0.0), axis=1, keepdims=True).astype(jnp.int32)
    cnt_ref[...] += jnp.sum(ohs, axis=0, keepdims=True)
    wout_ref[...] = jnp.where(lane == 0, p_group * (t1 / tot),
                              jnp.where(lane == 1, p_group * (t2 / tot), 0.0))
    eout_ref[...] = jnp.where(lane == 0, j1 - G, jnp.where(lane == 1, j2 - G,
                              jnp.where(lane == 2, r1, jnp.where(lane == 3, r2, 0))))


def _ln_router(x, mix, ln_g, ln_b, router_w1, router_b1, router_w2, router_b2, tm=256):
    n, d = x.shape
    ncol = MOE_GROUPS + MOE_GROUPS * MOE_PER_GROUP
    w = jnp.zeros((d, LANES), F32).at[:, :ncol].set(jnp.concatenate([router_w1, router_w2], axis=1))
    b = jnp.zeros((1, LANES), F32).at[0, :ncol].set(jnp.concatenate([router_b1, router_b2]))
    row = lambda width: pl.BlockSpec((tm, width), lambda i: (i, 0))
    const = lambda shape: pl.BlockSpec(shape, lambda i: (0, 0))
    return pl.pallas_call(
        _router_body,
        grid=(n // tm,),
        in_specs=[row(d), row(d), const((1, d)), const((1, d)), const((d, LANES)), const((d, LANES)),
                  const((1, LANES))],
        out_specs=[row(d), row(LANES), row(LANES), const((1, LANES))],
        out_shape=[jax.ShapeDtypeStruct((n, d), F32), jax.ShapeDtypeStruct((n, LANES), F32),
                   jax.ShapeDtypeStruct((n, LANES), jnp.int32), jax.ShapeDtypeStruct((1, LANES), F32)],
        compiler_params=_cparams("arbitrary"),
        name="ln_moe_router",
    )(x, mix, ln_g.reshape(1, d), ln_b.reshape(1, d), *_split_bf16(w), b)


def _row_copy(src_hbm, row, dst_vmem, r, sem):
    return pltpu.make_async_copy(src_hbm.at[pl.ds(row, 1), :], dst_vmem.at[pl.ds(r, 1), :], sem)


def _pack_bf16_pair(lo, hi):
    bits = lambda a: lax.bitcast_convert_type(a.astype(BF16).astype(F32), jnp.uint32)
    return (bits(hi) & jnp.uint32(0xFFFF0000)) | (bits(lo) >> 16)


def _unpack_bf16_pair(words):
    lo = lax.bitcast_convert_type(words << 16, F32)
    hi = lax.bitcast_convert_type(words & jnp.uint32(0xFFFF0000), F32)
    return lo, hi


def _expert_body(e_ref, nb_ref, nused_ref, tok_ref, x_hbm, w1_ref, w3_ref, w2_ref, w2b_ref, o_ref,
                 stage, xb, acc_a, acc_g, hb, sem, *, bm, rb, nkc):
    del e_ref, nused_ref
    i, c = pl.program_id(0), pl.program_id(1)
    nb = nb_ref[i]
    rmax = rb * bm
    tk = xb.shape[2]

    def for_rows(grp, fn):
        def block(b, carry):
            def body(r, carry2):
                row = b * bm + r
                fn(row, tok_ref[grp * rmax + row])
                return carry2
            return lax.fori_loop(0, bm, body, carry, unroll=8)
        lax.fori_loop(0, nb_ref[grp], block, 0)

    @pl.when(jnp.logical_and(i == 0, c == 0))
    def _():
        for_rows(0, lambda r, t: _row_copy(x_hbm, t, stage, r, sem).start())

    @pl.when(jnp.logical_and(c == 0, nb > 0))
    def _():
        for_rows(i, lambda r, t: _row_copy(x_hbm, 0, stage, r, sem).wait())
        for b in range(rb):
            q = pl.ds(b * bm, bm)

            @pl.when(b < nb)
            def _():
                for k in range(nkc):
                    xb[k, q, :] = stage[q, k * tk:(k + 1) * tk].astype(BF16)

            @pl.when(b >= nb)
            def _():
                for k in range(nkc):
                    xb[k, q, :] = jnp.zeros((bm, tk), BF16)

        @pl.when(i + 1 < pl.num_programs(0))
        def _():
            for_rows(i + 1, lambda r, t: _row_copy(x_hbm, t, stage, r, sem).start())

    def hidden(m):
        x = xb[c, 0:m, :]
        a = jnp.dot(x, w1_ref[...].astype(BF16), preferred_element_type=F32)
        g = jnp.dot(x, w3_ref[...].astype(BF16), preferred_element_type=F32)

        def finish(at, gt):
            hb[0:m, :] = ((at * jax.nn.sigmoid(at)) * gt).astype(BF16)

        if nkc == 1:
            finish(a, g)
            return

        @pl.when(c == 0)
        def _():
            acc_a[0:m, :] = a
            acc_g[0:m, :] = g

        @pl.when(jnp.logical_and(c > 0, c < nkc - 1))
        def _():
            acc_a[0:m, :] += a
            acc_g[0:m, :] += g

        @pl.when(c == nkc - 1)
        def _():
            finish(acc_a[0:m, :] + a, acc_g[0:m, :] + g)

    def project(m):
        h = hb[0:m, :]
        lo = jnp.dot(h, w2_ref[...].astype(BF16), preferred_element_type=F32)
        hi = jnp.dot(h, w2b_ref[...].astype(BF16), preferred_element_type=F32)
        o_ref[0:m, :] = _pack_bf16_pair(lo, hi)
        if m < rmax:
            o_ref[m:rmax, :] = jnp.zeros((rmax - m, o_ref.shape[1]), jnp.uint32)

    small = (rb - 1) * bm
    for cond, m in ((jnp.logical_and(nb > 0, nb < rb), small), (nb == rb, rmax)):
        @pl.when(jnp.logical_and(cond, c < nkc))
        def _():
            hidden(m)

        @pl.when(jnp.logical_and(cond, c >= nkc))
        def _():
            project(m)

    @pl.when(jnp.logical_and(nb == 0, c >= nkc))
    def _():
        o_ref[...] = jnp.zeros_like(o_ref)


def _experts(x, sb_e, sb_nb, n_used, slot_tok, w1, w3, w2, bm, rb):
    n, d = x.shape
    e, _, hd = w1.shape
    n_sb = sb_e.shape[0]
    rmax = rb * bm
    dh = d // 2
    tk, oc = min(MOE_KC, d), min(MOE_OC, dh)
    nkc, noc = d // tk, dh // oc

    def w13_map(i, c, se, nb, nu, tok):
        return se[i], jnp.where(i < nu[0], jnp.minimum(c, nkc - 1), nkc - 1), 0

    def w2_chunk(i, c, nu):
        return jnp.where(i < nu[0], jnp.clip(c - nkc, 0, noc - 1), noc - 1)

    def w2_map(i, c, se, nb, nu, tok):
        return se[i], 0, w2_chunk(i, c, nu)

    def w2b_map(i, c, se, nb, nu, tok):
        return se[i], 0, noc + w2_chunk(i, c, nu)

    def out_map(i, c, se, nb, nu, tok):
        writing = c >= nkc
        row = jnp.where(writing, i, jnp.maximum(i - 1, 0))
        col = jnp.where(writing, c - nkc, jnp.where(i == 0, 0, noc - 1))
        return row, col

    grid_spec = pltpu.PrefetchScalarGridSpec(
        num_scalar_prefetch=4,
        grid=(n_sb, nkc + noc),
        in_specs=[pl.BlockSpec(memory_space=pl.ANY),
                  pl.BlockSpec((None, tk, hd), w13_map),
                  pl.BlockSpec((None, tk, hd), w13_map),
                  pl.BlockSpec((None, hd, oc), w2_map),
                  pl.BlockSpec((None, hd, oc), w2b_map)],
        out_specs=pl.BlockSpec((rmax, oc), out_map),
        scratch_shapes=[pltpu.VMEM((rmax, d), F32), pltpu.VMEM((nkc, rmax, tk), BF16),
                        pltpu.VMEM((rmax, hd), F32), pltpu.VMEM((rmax, hd), F32),
                        pltpu.VMEM((rmax, hd), BF16), pltpu.SemaphoreType.DMA(())],
    )
    return pl.pallas_call(
        functools.partial(_expert_body, bm=bm, rb=rb, nkc=nkc),
        grid_spec=grid_spec,
        out_shape=jax.ShapeDtypeStruct((n_sb * rmax, dh), jnp.uint32),
        compiler_params=_cparams("arbitrary", "arbitrary"),
        name="moe_experts",
    )(sb_e, sb_nb, n_used, slot_tok, x, w1, w3, w2, w2)


def _combine_body(slot_ref, y_hbm, w_ref, x_ref, g_ref, b_ref, o_ref, buf, sem, *, tb):
    i = pl.program_id(0)

    def gather(tile, half):
        def issue(r, c):
            for kk in range(MOE_TOPK):
                _row_copy(y_hbm, slot_ref[(tile * tb + r) * MOE_TOPK + kk], buf.at[half, kk], r,
                          sem.at[half]).start()
            return c
        lax.fori_loop(0, tb, issue, 0, unroll=8)

    @pl.when(i == 0)
    def _():
        gather(0, 0)

    @pl.when(i + 1 < pl.num_programs(0))
    def _():
        gather(i + 1, (i + 1) % 2)

    half = i % 2

    def wait(r, c):
        for kk in range(MOE_TOPK):
            _row_copy(y_hbm, 0, buf.at[half, kk], r, sem.at[half]).wait()
        return c
    lax.fori_loop(0, tb, wait, 0, unroll=8)
    w = w_ref[...]
    lo0, hi0 = _unpack_bf16_pair(buf[half, 0])
    lo1, hi1 = _unpack_bf16_pair(buf[half, 1])
    y = jnp.concatenate([w[:, 0:1] * lo0 + w[:, 1:2] * lo1, w[:, 0:1] * hi0 + w[:, 1:2] * hi1], axis=1)
    o_ref[...] = _layer_norm(DEEPNORM_ALPHA * x_ref[...] + y, g_ref[...], b_ref[...])


def _combine_ln(slot_of, yb, weights, x, g, b, tb):
    n, d = x.shape
    grid_spec = pltpu.PrefetchScalarGridSpec(
        num_scalar_prefetch=1,
        grid=(n // tb,),
        in_specs=[pl.BlockSpec(memory_space=pl.ANY),
                  pl.BlockSpec((tb, LANES), lambda i, s: (i, 0)),
                  pl.BlockSpec((tb, d), lambda i, s: (i, 0)),
                  pl.BlockSpec((1, d), lambda i, s: (0, 0)),
                  pl.BlockSpec((1, d), lambda i, s: (0, 0))],
        out_specs=pl.BlockSpec((tb, d), lambda i, s: (i, 0)),
        scratch_shapes=[pltpu.VMEM((2, MOE_TOPK, tb, d // 2), jnp.uint32),
                        pltpu.SemaphoreType.DMA((2,))],
    )
    return pl.pallas_call(
        functools.partial(_combine_body, tb=tb),
        grid_spec=grid_spec,
        out_shape=jax.ShapeDtypeStruct((n, d), F32),
        compiler_params=_cparams("arbitrary"),
        name="moe_combine_ln",
    )(slot_of, yb, weights, x, g.reshape(1, d), b.reshape(1, d))


def _ln_moe_ln(x, mix, ln1_g, ln1_b, router_w1, router_b1, router_w2, router_b2, exp_w1, exp_w3, exp_w2,
               ln_g, ln_b):
    n, d = x.shape
    e = exp_w1.shape[0]
    bm = MOE_BM
    h1, weights, ids, cnt = _ln_router(x, mix, ln1_g, ln1_b, router_w1, router_b1, router_w2, router_b2)
    m = n * MOE_TOPK
    rb = MOE_RB
    rmax = rb * bm
    eid_f = ids[:, :MOE_TOPK].reshape(m)
    rank = ids[:, MOE_TOPK:2 * MOE_TOPK].reshape(m)
    counts = cnt[0, MOE_GROUPS:MOE_GROUPS + e].astype(jnp.int32)
    nblk_e = (counts + bm - 1) // bm
    ngrp_e = (nblk_e + rb - 1) // rb
    gend = jnp.cumsum(ngrp_e)
    gstart = gend - ngrp_e
    slot_of = ((gstart[eid_f] + rank // rmax) * rmax + rank % rmax).astype(jnp.int32)
    n_grp = -(-(-(-m // bm) + e * rb) // rb)
    tok_f = jnp.arange(m, dtype=jnp.int32) // MOE_TOPK
    slot_tok = jnp.zeros((n_grp * rmax,), jnp.int32).at[slot_of].set(tok_f)
    n_used = gend[-1:].astype(jnp.int32)
    gidx = jnp.arange(n_grp, dtype=jnp.int32)
    gcl = jnp.minimum(gidx, n_used[0] - 1)
    grp_e = jnp.minimum(jnp.sum(gend[None, :] <= gcl[:, None], axis=1), e - 1).astype(jnp.int32)
    local = gcl - gstart[grp_e]
    grp_nb = jnp.where(gidx < n_used[0], jnp.minimum(rb, nblk_e[grp_e] - local * rb), 0).astype(jnp.int32)
    yb = _experts(h1, grp_e, grp_nb, n_used, slot_tok, exp_w1, exp_w3, exp_w2, bm, rb)
    return _combine_ln(slot_of, yb, weights, h1, ln_g, ln_b, min(MOE_TB, n))


def kernel(x, w_in, hy_conv_w, hy_conv_b, hy_f_w1, hy_f_b1, hy_f_fr1, hy_f_w2, hy_f_b2, hy_f_fr2,
           hy_f_w3, hy_f_b3, hy_f_fr3, hy_f_wout, hy_bias, ml_gate_bias, ml_norm_g, p_hy, p_ml, w_out,
           ln1_g, ln1_b, router_w1, router_b1, router_w2, router_b2, exp_w1, exp_w3, exp_w2,
           ln2_g, ln2_b):
    B, L, D = x.shape
    N = B * L
    C = D // 2
    H = ML_HEADS
    dv = C // H
    dk = dv // 2
    col_q = (HY_ORDER + 1) * C
    col_if = col_q + 2 * H * dk + 2 * C
    col_gate = col_if + 4 * H

    xf = x.reshape(N, D)
    xb = xf.astype(BF16)
    w_hy = w_in[:, :col_q].astype(BF16)
    w_ml = w_in[:, col_q:col_if].astype(BF16)
    w_if = jnp.zeros((D, LANES), BF16).at[:, :4 * H].set(w_in[:, col_if:col_gate].astype(BF16))
    w_gt = w_in[:, col_gate:].astype(BF16)

    z_hy = _matmul(xb, w_hy, F32, 1024, 512).reshape(B, L, col_q)
    z_ml = _matmul(xb, w_ml, BF16, 1024, 512)
    z_if = _matmul(xb, w_if, F32, 1024, LANES)
    z_gt = _matmul(xb, w_gt, BF16, 1024, 512)

    tables, filter_tables = _dft_tables(L)
    kern = _hyena_filters(L, C, hy_f_w1, hy_f_b1, hy_f_fr1, hy_f_w2, hy_f_b2, hy_f_fr2,
                          hy_f_w3, hy_f_b3, hy_f_fr3, hy_f_wout, hy_bias)
    n2 = 2 * L // FFT_N1
    kfreq = _filter_fft(_row_permute(kern, FFT_N1, n2, F32), *filter_tables)
    u = _short_conv(z_hy, hy_conv_w, hy_conv_b, n2)
    v1 = _long_conv(u, 2, u, 0, kfreq, 0, tables, C)
    y_hy = _long_conv(v1, 0, u, 1, kfreq, 1, tables, C)
    y_hy = _row_permute(y_hy, n2, L // n2, BF16).reshape(N, C)

    g = z_if[:, :4 * H].reshape(B, L, 4, H)
    gates_c = g.transpose(0, 3, 1, 2)
    gates_r = g.transpose(0, 3, 2, 1)
    bias_c = ml_gate_bias.T.reshape(H, 1, 4)
    bias_r = ml_gate_bias.T.reshape(H, 4, 1)
    h_fwd = _mlstm(z_ml, gates_c, gates_r, bias_c, bias_r, B, L, dk, dv, rev=False)
    y_ml = _mlstm(z_ml, gates_c, gates_r, bias_c, bias_r, B, L, dk, dv, rev=True,
                  hprev=h_fwd, norm_g=ml_norm_g.reshape(1, C))

    merged = _merge(y_hy, y_ml, p_hy.astype(BF16), p_ml.astype(BF16), z_gt)
    mix = _matmul(merged, w_out.astype(BF16), BF16, 1024, 512)
    out = _ln_moe_ln(xf, mix, ln1_g, ln1_b, router_w1, router_b1, router_w2, router_b2,
                     exp_w1, exp_w3, exp_w2, ln2_g, ln2_b)
    return out.reshape(B, L, D)
```

```python
import functools
import math

import jax
import jax.numpy as jnp
from jax import lax
from jax.experimental import pallas as pl
from jax.experimental.pallas import tpu as pltpu

F32 = jnp.float32
BF16 = jnp.bfloat16
HIGHEST = lax.Precision.HIGHEST

VMEM_LIMIT_BYTES = 56 * 1024 * 1024
LANES = 128

HY_ORDER = 2
HY_SHORT = 3
HY_POS_EMB = 33
HY_DECAY_TARGET = 1e-2
HY_FAST_DECAY = 0.3
HY_SLOW_DECAY = 1.5
HY_MOD_SHIFT = 0.05
ML_HEADS = 8
MOE_GROUPS = 8
MOE_PER_GROUP = 8
MOE_TOPK = 2
DEPTH = 1
DEEPNORM_ALPHA = (2.0 * DEPTH) ** 0.25
LN_EPS = 1e-5

FFT_N1 = 64
HY_CB = 256
FFT_G = 32
FFT_KC = 16
FFT_UNROLL_OUTER = 4
FFT_UNROLL_INNER = 4
ML_CHUNK = 256
ML_CHUNKS_PER_STEP = 4
MOE_BM = 128
MOE_RB = 5
MOE_KC = 1024
MOE_OC = 512
MOE_TB = 256


def _cparams(*sem):
    return pltpu.CompilerParams(dimension_semantics=sem, vmem_limit_bytes=VMEM_LIMIT_BYTES)


def _mm_body(a_ref, b_ref, o_ref):
    o_ref[...] = jnp.dot(a_ref[...], b_ref[...], preferred_element_type=F32).astype(o_ref.dtype)


def _matmul(a, b, out_dtype, tm, tn):
    m, k = a.shape
    _, n = b.shape
    assert m % tm == 0 and n % tn == 0
    return pl.pallas_call(
        _mm_body,
        grid=(m // tm, n // tn),
        in_specs=[pl.BlockSpec((tm, k), lambda i, j: (i, 0)),
                  pl.BlockSpec((k, tn), lambda i, j: (0, j))],
        out_specs=pl.BlockSpec((tm, tn), lambda i, j: (i, j)),
        out_shape=jax.ShapeDtypeStruct((m, n), out_dtype),
        compiler_params=_cparams("parallel", "arbitrary"),
        name="proj_matmul",
    )(a, b)


def _permute_pitch(outer):
    return outer + 8


def _permute_rows(get_rows, tmp_ref, dst_ref, inner, outer):
    pitch = _permute_pitch(outer)
    for b in range(inner):
        tmp_ref[0, b * pitch:b * pitch + outer, :] = get_rows(b * outer, (b + 1) * outer)

    def it(a, c):
        rows = tmp_ref[0, pl.ds(a, inner, stride=pitch), :]
        dst_ref[0, pl.ds(pl.multiple_of(a * inner, inner), inner), :] = rows.astype(dst_ref.dtype)
        return c
    lax.fori_loop(0, outer, it, 0, unroll=4)


def _short_conv_body(z_ref, w_ref, b_ref, o_ref, tmp, *, n2):
    z = z_ref[0]
    L = z.shape[0]
    w0, w1, w2 = w_ref[0:1, :], w_ref[1:2, :], w_ref[2:3, :]
    u = b_ref[...] + pltpu.roll(z, 1, 0) * w0 + z * w1 + pltpu.roll(z, L - 1, 0) * w2
    r8 = lax.broadcasted_iota(jnp.int32, (8, 1), 0)
    head = u[0:8] - jnp.where(r8 == 0, z[L - 1:L] * w0, 0.0)
    tail = u[L - 8:L] - jnp.where(r8 == 7, z[0:1] * w2, 0.0)

    def rows(lo, hi):
        if lo == 0:
            return jnp.concatenate([head, u[8:hi]], axis=0)
        if hi == L:
            return jnp.concatenate([u[lo:L - 8], tail], axis=0)
        return u[lo:hi]
    _permute_rows(rows, tmp, o_ref, L // n2, n2)


def _short_conv(z, w, b, n2, cb=LANES):
    B, L, C = z.shape
    return pl.pallas_call(
        functools.partial(_short_conv_body, n2=n2),
        grid=(B, C // cb),
        in_specs=[pl.BlockSpec((1, L, cb), lambda i, j: (i, 0, j)),
                  pl.BlockSpec((HY_SHORT, cb), lambda i, j: (0, j)),
                  pl.BlockSpec((1, cb), lambda i, j: (0, j))],
        out_specs=pl.BlockSpec((1, L, cb), lambda i, j: (i, 0, j)),
        out_shape=jax.ShapeDtypeStruct((B, L, C), F32),
        scratch_shapes=[pltpu.VMEM((1, (L // n2) * _permute_pitch(n2), cb), F32)],
        compiler_params=_cparams("parallel", "parallel"),
        name="hy_short_conv",
    )(z, w, b.reshape(1, C))


def _row_permute_body(x_ref, o_ref, tmp, *, inner, outer):
    _permute_rows(lambda lo, hi: x_ref[0, lo:hi, :], tmp, o_ref, inner, outer)


def _row_permute(x, inner, outer, out_dtype, cb=LANES):
    A, R, C = x.shape
    assert R == inner * outer
    cb = min(cb, C)
    return pl.pallas_call(
        functools.partial(_row_permute_body, inner=inner, outer=outer),
        grid=(A, C // cb),
        in_specs=[pl.BlockSpec((1, R, cb), lambda i, j: (i, 0, j))],
        out_specs=pl.BlockSpec((1, R, cb), lambda i, j: (i, 0, j)),
        out_shape=jax.ShapeDtypeStruct((A, R, C), out_dtype),
        scratch_shapes=[pltpu.VMEM((1, inner * _permute_pitch(outer), cb), F32)],
        compiler_params=_cparams("parallel", "parallel"),
        name="hy_row_permute",
    )(x)


def _filter_hidden_body(w1_ref, b1_ref, fr1_ref, w2_ref, b2_ref, fr2_ref, w3_ref, b3_ref, fr3_ref,
                        o_ref, *, L, rows):
    i = pl.program_id(0)
    n = i * rows + lax.broadcasted_iota(jnp.int32, (rows, 1), 0)
    pos = jnp.where(n < L, n, 2 * L - n).astype(F32)
    t = pos / (L - 1.0)
    w = (2.0 * math.pi / L) * pos
    lane = lax.broadcasted_iota(jnp.int32, (1, LANES), 1)
    bands = (HY_POS_EMB - 1) // 2
    band = jnp.where(lane <= bands, lane - 1, lane - 1 - bands).astype(F32)
    freq = 1e-4 + band * ((bands - 1 - 1e-4) / (bands - 1))
    ang = w * freq
    feats = jnp.where(lane == 0, t,
                      jnp.where(lane <= bands, jnp.cos(ang),
                                jnp.where(lane <= 2 * bands, -jnp.sin(ang), 0.0)))
    h = jnp.sin(fr1_ref[...] * (jnp.dot(feats, w1_ref[...], precision=HIGHEST,
                                        preferred_element_type=F32) + b1_ref[...]))
    h = jnp.sin(fr2_ref[...] * (jnp.dot(h, w2_ref[...], precision=HIGHEST,
                                        preferred_element_type=F32) + b2_ref[...]))
    h = jnp.sin(fr3_ref[...] * (jnp.dot(h, w3_ref[...], precision=HIGHEST,
                                        preferred_element_type=F32) + b3_ref[...]))
    o_ref[...] = h


def _filter_out_body(h_ref, wout_ref, delta_ref, bias_ref, o_ref, *, L):
    d = pl.program_id(1)
    h = _dot3_split(h_ref[...], wout_ref[0])
    r = lax.broadcasted_iota(jnp.int32, (L, 1), 0)
    pos = jnp.where(d == 0, r, L - r).astype(F32)
    t = pos / (L - 1.0)
    window = jnp.exp(-t * delta_ref[...]) + HY_MOD_SHIFT
    first = r == 0
    tap = jnp.where(jnp.logical_and(d == 1, first), 0.0, h * window)
    o_ref[0] = tap + jnp.where(jnp.logical_and(d == 0, first), bias_ref[...], 0.0)


def _hyena_filters(L, C, f_w1, f_b1, f_fr1, f_w2, f_b2, f_fr2, f_w3, f_b3, f_fr3, f_wout, bias, cb=512):
    fh = f_w2.shape[0]
    rows = 1024 if (2 * L) % 1024 == 0 else 2 * L
    w1p = jnp.zeros((LANES, fh), F32).at[:HY_POS_EMB].set(f_w1)
    vec = lambda a: a.reshape(1, fh)
    full = lambda shape: pl.BlockSpec(shape, lambda i: (0,) * len(shape))
    hid = pl.pallas_call(
        functools.partial(_filter_hidden_body, L=L, rows=rows),
        grid=(2 * L // rows,),
        in_specs=[full((LANES, fh)), full((1, fh)), full((1, fh)),
                  full((fh, fh)), full((1, fh)), full((1, fh)),
                  full((fh, fh)), full((1, fh)), full((1, fh))],
        out_specs=pl.BlockSpec((rows, fh), lambda i: (i, 0)),
        out_shape=jax.ShapeDtypeStruct((2 * L, fh), F32),
        compiler_params=_cparams("parallel"),
        name="hy_filter_hidden",
    )(w1p, vec(f_b1), vec(f_fr1), f_w2, vec(f_b2), vec(f_fr2), f_w3, vec(f_b3), vec(f_fr3))
    wout = f_wout.reshape(fh, 2, HY_ORDER, C).transpose(1, 2, 0, 3).reshape(2 * HY_ORDER, fh, C)
    deltas = jnp.abs(jnp.linspace(math.log(HY_DECAY_TARGET) / HY_SLOW_DECAY,
                                  math.log(HY_DECAY_TARGET) / HY_FAST_DECAY, C, dtype=F32))
    cb = min(cb, C)
    return pl.pallas_call(
        functools.partial(_filter_out_body, L=L),
        grid=(HY_ORDER, 2, C // cb),
        in_specs=[pl.BlockSpec((L, fh), lambda o, d, j: (d, 0)),
                  pl.BlockSpec((1, fh, cb), lambda o, d, j: (d * HY_ORDER + o, 0, j)),
                  pl.BlockSpec((1, cb), lambda o, d, j: (0, j)),
                  pl.BlockSpec((None, 1, cb), lambda o, d, j: (o, 0, j))],
        out_specs=pl.BlockSpec((1, L, cb), lambda o, d, j: (o, d, j)),
        out_shape=jax.ShapeDtypeStruct((HY_ORDER, 2 * L, C), F32),
        compiler_params=_cparams("parallel", "parallel", "parallel"),
        name="hy_filter_out",
    )(hid, wout, deltas.reshape(1, C), bias.reshape(HY_ORDER, 1, C))


def _dft_tables(L):
    N = 2 * L
    N1 = FFT_N1
    N2 = N // N1
    h = N1 // 2
    n2 = jnp.arange(N2, dtype=jnp.int32)[:, None, None]
    k1 = jnp.arange(N1, dtype=jnp.int32)[None, :, None]
    n1 = jnp.arange(N1, dtype=jnp.int32)[None, None, :]
    ph = (k1 * (N2 * n1 + n2)) % N
    ang = ph.astype(F32) * (-2.0 * math.pi / N)
    mr, mi = jnp.cos(ang), jnp.sin(ang)
    mrp, mip = mr[:, :, :h], mi[:, :, :h]
    t1 = jnp.concatenate([jnp.concatenate([mrp, -mip], axis=2),
                          jnp.concatenate([mip, mrp], axis=2)], axis=1)
    mrt, mit = jnp.swapaxes(mrp, 1, 2) / N, jnp.swapaxes(mip, 1, 2) / N
    t1i = jnp.concatenate([jnp.concatenate([mrt, mit], axis=2),
                           jnp.concatenate([-mit, mrt], axis=2)], axis=1)
    t1f = jnp.concatenate([mr, mi], axis=1)
    a = jnp.arange(N2, dtype=jnp.int32)
    ang2 = ((a[:, None] * a[None, :]) % N2).astype(F32) * (-2.0 * math.pi / N2)
    fr, fi = jnp.cos(ang2), jnp.sin(ang2)
    t2 = jnp.concatenate([jnp.concatenate([fr, -fi], axis=1),
                          jnp.concatenate([fi, fr], axis=1)], axis=0)
    t2i = jnp.concatenate([jnp.concatenate([fr, fi], axis=1),
                           jnp.concatenate([-fi, fr], axis=1)], axis=0)
    conv_tables = tuple(t.astype(BF16) for t in (t1, t1i, t2, t2i))
    filter_tables = (_hi_lo_rows(t1f), _hi_lo_rows(t2))
    return conv_tables, filter_tables


def _split_bf16(d):
    hi = d.astype(BF16)
    return hi, (d - hi.astype(F32)).astype(BF16)


def _dot3_split(a, b):
    a_hi, a_lo = _split_bf16(a)
    b_hi, b_lo = _split_bf16(b)
    return (jnp.dot(a_hi, b_hi, preferred_element_type=F32)
            + jnp.dot(a_lo, b_hi, preferred_element_type=F32)
            + jnp.dot(a_hi, b_lo, preferred_element_type=F32))


def _hi_lo_rows(t):
    hi, lo = _split_bf16(t)
    return jnp.concatenate([hi, lo], axis=-2)


def _ld_lanes(ref, rows):
    return jnp.concatenate([ref[i, rows, :] for i in range(ref.shape[0])], axis=1)


def _st_lanes(ref, rows, val):
    for i in range(ref.shape[0]):
        ref[i, rows, :] = val[:, i * LANES:(i + 1) * LANES]


def _dot3(t, d, m):
    d_hi, d_lo = _split_bf16(d)
    y = jnp.dot(t, d_hi, preferred_element_type=F32)
    return y[:m] + y[m:] + jnp.dot(t[:m], d_lo, preferred_element_type=F32)


def _dot1(t, d):
    return jnp.dot(t, d.astype(BF16), preferred_element_type=F32)


def _work_pitch(N2):
    return N2 + 8


def _fft_steps(N1, N2):
    return min(FFT_G, N2), min(FFT_KC, N1)


def _filter_fft_body(k_ref, t1f_ref, t2_ref, o_ref, sr, si, *, N1, N2, G, KC):
    s = pl.program_id(2)
    sa = N2 // G
    P = _work_pitch(N2)

    @pl.when(s < sa)
    def _():
        def it(r, c):
            n2 = s * G + r
            rows = k_ref[pl.ds(pl.multiple_of(r * N1, N1), N1), :]
            a = _dot3(t1f_ref[n2], rows, 2 * N1)
            _st_lanes(sr, pl.ds(n2, N1, stride=P), a[:N1])
            _st_lanes(si, pl.ds(n2, N1, stride=P), a[N1:])
            return c
        lax.fori_loop(0, G, it, 0, unroll=FFT_UNROLL_OUTER)

    @pl.when(s >= sa)
    def _():
        t2 = t2_ref[...]

        def it(kk, c):
            r0 = pl.multiple_of(((s - sa) * KC + kk) * P, 8)
            q0 = pl.multiple_of(kk * N2, N2)
            x = _dot3(t2, jnp.concatenate([_ld_lanes(sr, pl.ds(r0, N2)), _ld_lanes(si, pl.ds(r0, N2))],
                                          axis=0), 2 * N2)
            o_ref[0, pl.ds(q0, N2), :] = x[:N2]
            o_ref[1, pl.ds(q0, N2), :] = x[N2:]
            return c
        lax.fori_loop(0, KC, it, 0, unroll=FFT_UNROLL_INNER)


def _filter_fft(kern, t1f, t2):
    O, N, C = kern.shape
    N1, N2 = FFT_N1, N // FFT_N1
    G, KC = _fft_steps(N1, N2)
    sa, sb = N2 // G, N1 // KC
    cb = min(HY_CB, C)
    full = lambda shape: pl.BlockSpec(shape, lambda o, j, s: (0,) * len(shape),
                                      pipeline_mode=pl.Buffered(1))
    return pl.pallas_call(
        functools.partial(_filter_fft_body, N1=N1, N2=N2, G=G, KC=KC),
        grid=(O, C // cb, sa + sb),
        in_specs=[pl.BlockSpec((None, G * N1, cb), lambda o, j, s: (o, jnp.minimum(s, sa - 1), j)),
                  full(t1f.shape), full(t2.shape)],
        out_specs=pl.BlockSpec((None, 2, KC * N2, cb),
                               lambda o, j, s: (o, 0, jnp.maximum(s - sa, 0), j)),
        out_shape=jax.ShapeDtypeStruct((O, 2, N, C), F32),
        scratch_shapes=[pltpu.VMEM((cb // LANES, N1 * _work_pitch(N2), LANES), F32)] * 2,
        compiler_params=_cparams("parallel", "parallel", "arbitrary"),
        name="hy_filter_fft",
    )(kern, t1f, t2)


def _long_conv_body(v_ref, g_ref, kf_ref, t1_ref, t1i_ref, t2_ref, t2i_ref, o_ref, sr, si,
                    *, N1, N2, G, KC):
    s = pl.program_id(2)
    sa, sb = N2 // G, N1 // KC
    h = N1 // 2
    P = _work_pitch(N2)

    @pl.when(s < sa)
    def _():
        def it(r, c):
            n2 = s * G + r
            q = pl.ds(pl.multiple_of(r * h, h), h)
            d = jnp.concatenate([v_ref[0, q, :], v_ref[1, q, :]], axis=0)
            a = _dot1(t1_ref[n2], d)
            _st_lanes(sr, pl.ds(n2, N1, stride=P), a[:N1])
            _st_lanes(si, pl.ds(n2, N1, stride=P), a[N1:])
            return c
        lax.fori_loop(0, G, it, 0, unroll=FFT_UNROLL_OUTER)

    @pl.when(jnp.logical_and(s >= sa, s < sa + sb))
    def _():
        t2, t2i = t2_ref[...], t2i_ref[...]

        def it(kk, c):
            r0 = pl.multiple_of(((s - sa) * KC + kk) * P, 8)
            q0 = pl.multiple_of(kk * N2, N2)
            x = _dot1(t2, jnp.concatenate([_ld_lanes(sr, pl.ds(r0, N2)), _ld_lanes(si, pl.ds(r0, N2))],
                                          axis=0))
            xr, xi = x[:N2], x[N2:]
            kr = kf_ref[0, pl.ds(q0, N2), :]
            ki = kf_ref[1, pl.ds(q0, N2), :]
            p = jnp.concatenate([xr * kr - xi * ki, xr * ki + xi * kr], axis=0)
            y = _dot1(t2i, p)
            _st_lanes(sr, pl.ds(r0, N2), y[:N2])
            _st_lanes(si, pl.ds(r0, N2), y[N2:])
            return c
        lax.fori_loop(0, KC, it, 0, unroll=FFT_UNROLL_INNER)

    @pl.when(s >= sa + sb)
    def _():
        def it(r, c):
            n2 = (s - sa - sb) * G + r
            q = pl.ds(pl.multiple_of(r * h, h), h)
            d = jnp.concatenate([_ld_lanes(sr, pl.ds(n2, N1, stride=P)),
                                 _ld_lanes(si, pl.ds(n2, N1, stride=P))], axis=0)
            y = _dot1(t1i_ref[n2], d)
            for b in range(2):
                o_ref[b, q, :] = g_ref[b, q, :] * y[b * h:(b + 1) * h]
            return c
        lax.fori_loop(0, G, it, 0, unroll=FFT_UNROLL_OUTER)


def _long_conv(v_arr, v_off, g_arr, g_off, kfreq, order, tables, C):
    B, L, _ = v_arr.shape
    t1, t1i, t2, t2i = tables
    N = 2 * L
    N1, N2 = FFT_N1, N // FFT_N1
    G, KC = _fft_steps(N1, N2)
    sa, sb = N2 // G, N1 // KC
    h = N1 // 2
    cb = min(HY_CB, C)
    nj = C // cb
    full = lambda shape: pl.BlockSpec(shape, lambda j, p, s: (0,) * len(shape),
                                      pipeline_mode=pl.Buffered(1))
    last = lambda s: jnp.clip(s - sa - sb, 0, sa - 1)
    return pl.pallas_call(
        functools.partial(_long_conv_body, N1=N1, N2=N2, G=G, KC=KC),
        grid=(nj, B // 2, 2 * sa + sb),
        in_specs=[pl.BlockSpec((2, G * h, cb), lambda j, p, s: (p, jnp.minimum(s, sa - 1), v_off * nj + j)),
                  pl.BlockSpec((2, G * h, cb), lambda j, p, s: (p, last(s), g_off * nj + j)),
                  pl.BlockSpec((None, 2, KC * N2, cb),
                               lambda j, p, s: (order, 0, jnp.clip(s - sa, 0, sb - 1), j)),
                  full(t1.shape), full(t1i.shape), full(t2.shape), full(t2i.shape)],
        out_specs=pl.BlockSpec((2, G * h, cb), lambda j, p, s: (p, last(s), j)),
        out_shape=jax.ShapeDtypeStruct((B, L, C), F32),
        scratch_shapes=[pltpu.VMEM((cb // LANES, N1 * _work_pitch(N2), LANES), F32)] * 2,
        compiler_params=_cparams("parallel", "arbitrary", "arbitrary"),
        name="hy_long_conv",
    )(v_arr, g_arr, kfreq, t1, t1i, t2, t2i)


def _log_sigmoid(x):
    return jnp.minimum(x, 0.0) - jnp.log1p(jnp.exp(-jnp.abs(x)))


def _mlstm_body(*refs, rev, T, nsub, scale, final):
    if final:
        (q_ref, k_ref, v_ref, gc_ref, gr_ref, bc_ref, br_ref, hprev_ref, o_ref, ng_ref,
         out_ref, c_s, n_s, m_s) = refs
    else:
        q_ref, k_ref, v_ref, gc_ref, gr_ref, bc_ref, br_ref, out_ref, c_s, n_s, m_s = refs

    @pl.when(pl.program_id(2) == 0)
    def _():
        c_s[...] = jnp.zeros_like(c_s)
        n_s[...] = jnp.zeros_like(n_s)
        m_s[...] = jnp.zeros_like(m_s)

    gi = 2 if rev else 0
    row = lax.broadcasted_iota(jnp.int32, (T, T), 0)
    col = lax.broadcasted_iota(jnp.int32, (T, T), 1)
    valid = (col >= row) if rev else (col <= row)
    valid_t = (row >= col) if rev else (row <= col)

    def chunk(j):
        rows = slice(j * T, (j + 1) * T)
        gc = gc_ref[rows, :] + bc_ref[...]
        gr = gr_ref[:, rows] + br_ref[...]
        li_c, lf_c = gc[:, gi:gi + 1], _log_sigmoid(gc[:, gi + 1:gi + 2])
        li_r, lf_r = gr[gi:gi + 1, :], _log_sigmoid(gr[gi + 1:gi + 2, :])
        b_c = jnp.sum(jnp.where(valid, lf_r, 0.0), axis=1, keepdims=True)
        b_r = jnp.sum(jnp.where(valid_t, lf_c, 0.0), axis=0, keepdims=True)
        m = m_s[...]
        d = jnp.where(valid, b_c - b_r + li_r, -jnp.inf)
        inter = b_c + m
        m_t = jnp.maximum(inter, jnp.max(d, axis=1, keepdims=True))
        q, k, v = q_ref[rows, :], k_ref[rows, :], v_ref[rows, :]
        qk = lax.dot_general(q, k, (((1,), (1,)), ((), ())), preferred_element_type=F32)
        s = qk * scale * jnp.exp(d - m_t)
        w_inter = jnp.exp(inter - m_t)
        qc = jnp.dot(q, c_s[...].astype(BF16), preferred_element_type=F32) * scale
        num = jnp.dot(s.astype(BF16), v, preferred_element_type=F32) + w_inter * qc
        qn = jnp.sum(q.astype(F32) * n_s[...], axis=1, keepdims=True) * scale
        den = jnp.sum(s, axis=1, keepdims=True) + w_inter * qn
        hout = num / jnp.maximum(jnp.abs(den), jnp.exp(-m_t))

        b_last = b_c[0:1, :] if rev else b_c[T - 1:T, :]
        w_c = b_last - b_c + li_c
        m_new = jnp.maximum(b_last + m, jnp.max(w_c, axis=0, keepdims=True))
        kw = k.astype(F32) * jnp.exp(w_c - m_new)
        decay = jnp.exp(b_last + m - m_new)
        c_s[...] = decay * c_s[...] + lax.dot_general(
            kw.astype(BF16), v, (((0,), (0,)), ((), ())), preferred_element_type=F32)
        n_s[...] = decay * n_s[...] + jnp.sum(kw, axis=0, keepdims=True)
        m_s[...] = m_new

        if final:
            hsum = hout + hprev_ref[rows, :]
            mu = jnp.mean(hsum, axis=1, keepdims=True)
            var = jnp.mean(jnp.square(hsum - mu), axis=1, keepdims=True)
            hn = (hsum - mu) * lax.rsqrt(var + LN_EPS) * ng_ref[...]
            out_ref[rows, :] = (jax.nn.sigmoid(o_ref[rows, :].astype(F32)) * hn).astype(out_ref.dtype)
        else:
            out_ref[rows, :] = hout

    for j in (reversed(range(nsub)) if rev else range(nsub)):
        chunk(j)


def _mlstm(zq, gates_c, gates_r, bias_c, bias_r, B, L, dk, dv, rev, hprev=None, norm_g=None):
    H = ML_HEADS
    T = min(ML_CHUNK, L)
    nsub = min(ML_CHUNKS_PER_STEP, L // T)
    tb = T * nsub
    nc = L // tb
    final = hprev is not None
    cidx = (lambda c: nc - 1 - c) if rev else (lambda c: c)
    kq, kv = H * dk // dk, (2 * H * dk) // dv
    in_specs = [
        pl.BlockSpec((tb, dk), lambda b, h, c: (b * nc + cidx(c), h)),
        pl.BlockSpec((tb, dk), lambda b, h, c: (b * nc + cidx(c), kq + h)),
        pl.BlockSpec((tb, dv), lambda b, h, c: (b * nc + cidx(c), kv + h)),
        pl.BlockSpec((None, None, tb, 4), lambda b, h, c: (b, h, cidx(c), 0)),
        pl.BlockSpec((None, None, 4, tb), lambda b, h, c: (b, h, 0, cidx(c))),
        pl.BlockSpec((None, 1, 4), lambda b, h, c: (h, 0, 0)),
        pl.BlockSpec((None, 4, 1), lambda b, h, c: (h, 0, 0)),
    ]
    args = [zq, zq, zq, gates_c, gates_r, bias_c, bias_r]
    if final:
        in_specs += [
            pl.BlockSpec((tb, dv), lambda b, h, c: (b * nc + cidx(c), h)),
            pl.BlockSpec((tb, dv), lambda b, h, c: (b * nc + cidx(c), kv + H + h)),
            pl.BlockSpec((1, dv), lambda b, h, c: (0, h)),
        ]
        args += [hprev, zq, norm_g]
    return pl.pallas_call(
        functools.partial(_mlstm_body, rev=rev, T=T, nsub=nsub, scale=dk ** -0.5, final=final),
        grid=(B, H, nc),
        in_specs=in_specs,
        out_specs=pl.BlockSpec((tb, dv), lambda b, h, c: (b * nc + cidx(c), h)),
        out_shape=jax.ShapeDtypeStruct((B * L, H * dv), BF16 if final else F32),
        scratch_shapes=[pltpu.VMEM((dk, dv), F32), pltpu.VMEM((1, dk), F32), pltpu.VMEM((1, 1), F32)],
        compiler_params=_cparams("parallel", "parallel", "arbitrary"),
        name="mlstm_bwd" if rev else "mlstm_fwd",
    )(*args)


def _merge_body(yh_ref, ym_ref, ph_ref, pm_ref, gh_ref, gm_ref, o_ref):
    a = jnp.dot(yh_ref[...], ph_ref[...], preferred_element_type=F32)
    b = jnp.dot(ym_ref[...], pm_ref[...], preferred_element_type=F32)
    o_ref[...] = (jax.nn.sigmoid(gh_ref[...].astype(F32)) * a
                  + jax.nn.sigmoid(gm_ref[...].astype(F32)) * b).astype(o_ref.dtype)


def _merge(y_hy, y_ml, p_hy, p_ml, gates, tm=1024, tn=512):
    n, kh = y_hy.shape
    km = y_ml.shape[1]
    d = p_hy.shape[1]
    nj = d // tn
    return pl.pallas_call(
        _merge_body,
        grid=(n // tm, nj),
        in_specs=[pl.BlockSpec((tm, kh), lambda i, j: (i, 0)),
                  pl.BlockSpec((tm, km), lambda i, j: (i, 0)),
                  pl.BlockSpec((kh, tn), lambda i, j: (0, j)),
                  pl.BlockSpec((km, tn), lambda i, j: (0, j)),
                  pl.BlockSpec((tm, tn), lambda i, j: (i, j)),
                  pl.BlockSpec((tm, tn), lambda i, j: (i, nj + j))],
        out_specs=pl.BlockSpec((tm, tn), lambda i, j: (i, j)),
        out_shape=jax.ShapeDtypeStruct((n, d), BF16),
        compiler_params=_cparams("parallel", "arbitrary"),
        name="gated_merge",
    )(y_hy, y_ml, p_hy, p_ml, gates, gates)


def _layer_norm(x, g, b):
    mu = jnp.mean(x, axis=-1, keepdims=True)
    var = jnp.mean(jnp.square(x - mu), axis=-1, keepdims=True)
    return (x - mu) * lax.rsqrt(var + LN_EPS) * g + b


def _router_body(x_ref, mix_ref, lg_ref, lb_ref, whi_ref, wlo_ref, b_ref, h_ref, wout_ref, eout_ref,
                 cnt_ref):
    G, PG = MOE_GROUPS, MOE_PER_GROUP

    @pl.when(pl.program_id(0) == 0)
    def _():
        cnt_ref[...] = jnp.zeros_like(cnt_ref)

    h = _layer_norm(DEEPNORM_ALPHA * x_ref[...] + mix_ref[...].astype(F32), lg_ref[...], lb_ref[...])
    h_ref[...] = h
    h_hi, h_lo = _split_bf16(h)
    logits = (jnp.dot(h_hi, whi_ref[...], preferred_element_type=F32)
              + jnp.dot(h_lo, whi_ref[...], preferred_element_type=F32)
              + jnp.dot(h_hi, wlo_ref[...], preferred_element_type=F32)) + b_ref[...]
    lane = lax.broadcasted_iota(jnp.int32, logits.shape, 1)
    ninf = -jnp.inf
    first = lambda mask: jnp.min(jnp.where(mask, lane, 2 * LANES), axis=1, keepdims=True)
    lg1 = jnp.where(lane < G, logits, ninf)
    m1 = jnp.max(lg1, axis=1, keepdims=True)
    g_sel = first(lg1 == m1)
    p_group = 1.0 / jnp.sum(jnp.exp(lg1 - m1), axis=1, keepdims=True)
    lo = G + g_sel * PG
    in_grp = jnp.logical_and(lane >= lo, lane < lo + PG)
    lg2 = jnp.where(in_grp, logits, ninf)
    m2 = jnp.max(lg2, axis=1, keepdims=True)
    e2 = jnp.exp(lg2 - m2)
    p2 = jnp.where(in_grp, e2 / jnp.sum(e2, axis=1, keepdims=True), -1.0)
    t1 = jnp.max(p2, axis=1, keepdims=True)
    j1 = first(p2 == t1)
    p2b = jnp.where(lane == j1, -1.0, p2)
    t2 = jnp.max(p2b, axis=1, keepdims=True)
    j2 = first(p2b == t2)
    tot = t1 + t2
    oh1, oh2 = lane == j1, lane == j2
    ohs = jnp.where(jnp.logical_or(oh1, oh2), 1.0, 0.0)
    tm = ohs.shape[0]
    earlier = (lax.broadcasted_iota(jnp.int32, (tm, tm), 1)
               < lax.broadcasted_iota(jnp.int32, (tm, tm), 0))
    base = cnt_ref[...] + jnp.dot(jnp.where(earlier, 1.0, 0.0).astype(BF16), ohs.astype(BF16),
                                  preferred_element_type=F32)
    r1 = jnp.sum(jnp.where(oh1, base, 0.0), axis=1, keepdims=True).astype(jnp.int32)
    r2 = jnp.sum(jnp.where(oh2, base, 0.0), axis=1, keepdims=True).astype(jnp.int32)
    cnt_ref[...] += jnp.sum(ohs, axis=0, keepdims=True)
    wout_ref[...] = jnp.where(lane == 0, p_group * (t1 / tot),
                              jnp.where(lane == 1, p_group * (t2 / tot), 0.0))
    eout_ref[...] = jnp.where(lane == 0, j1 - G, jnp.where(lane == 1, j2 - G,
                              jnp.where(lane == 2, r1, jnp.where(lane == 3, r2, 0))))


def _ln_router(x, mix, ln_g, ln_b, router_w1, router_b1, router_w2, router_b2, tm=256):
    n, d = x.shape
    ncol = MOE_GROUPS + MOE_GROUPS * MOE_PER_GROUP
    w = jnp.zeros((d, LANES), F32).at[:, :ncol].set(jnp.concatenate([router_w1, router_w2], axis=1))
    b = jnp.zeros((1, LANES), F32).at[0, :ncol].set(jnp.concatenate([router_b1, router_b2]))
    row = lambda width: pl.BlockSpec((tm, width), lambda i: (i, 0))
    const = lambda shape: pl.BlockSpec(shape, lambda i: (0, 0))
    return pl.pallas_call(
        _router_body,
        grid=(n // tm,),
        in_specs=[row(d), row(d), const((1, d)), const((1, d)), const((d, LANES)), const((d, LANES)),
                  const((1, LANES))],
        out_specs=[row(d), row(LANES), row(LANES), const((1, LANES))],
        out_shape=[jax.ShapeDtypeStruct((n, d), F32), jax.ShapeDtypeStruct((n, LANES), F32),
                   jax.ShapeDtypeStruct((n, LANES), jnp.int32), jax.ShapeDtypeStruct((1, LANES), F32)],
        compiler_params=_cparams("arbitrary"),
        name="ln_moe_router",
    )(x, mix, ln_g.reshape(1, d), ln_b.reshape(1, d), *_split_bf16(w), b)


def _row_copy(src_hbm, row, dst_vmem, r, sem):
    return pltpu.make_async_copy(src_hbm.at[pl.ds(row, 1), :], dst_vmem.at[pl.ds(r, 1), :], sem)


def _pack_bf16_pair(lo, hi):
    bits = lambda a: lax.bitcast_convert_type(a.astype(BF16).astype(F32), jnp.uint32)
    return (bits(hi) & jnp.uint32(0xFFFF0000)) | (bits(lo) >> 16)


def _unpack_bf16_pair(words):
    lo = lax.bitcast_convert_type(words << 16, F32)
    hi = lax.bitcast_convert_type(words & jnp.uint32(0xFFFF0000), F32)
    return lo, hi


def _expert_body(e_ref, nb_ref, nused_ref, tok_ref, x_hbm, w1_ref, w3_ref, w2_ref, w2b_ref, o_ref,
                 stage, xb, acc_a, acc_g, hb, sem, *, bm, rb, nkc):
    del e_ref, nused_ref
    i, c = pl.program_id(0), pl.program_id(1)
    nb = nb_ref[i]
    rmax = rb * bm
    tk = xb.shape[2]

    def for_rows(grp, fn):
        def block(b, carry):
            def body(r, carry2):
                row = b * bm + r
                fn(row, tok_ref[grp * rmax + row])
                return carry2
            return lax.fori_loop(0, bm, body, carry, unroll=8)
        lax.fori_loop(0, nb_ref[grp], block, 0)

    @pl.when(jnp.logical_and(i == 0, c == 0))
    def _():
        for_rows(0, lambda r, t: _row_copy(x_hbm, t, stage, r, sem).start())

    @pl.when(jnp.logical_and(c == 0, nb > 0))
    def _():
        for_rows(i, lambda r, t: _row_copy(x_hbm, 0, stage, r, sem).wait())
        for b in range(rb):
            q = pl.ds(b * bm, bm)

            @pl.when(b < nb)
            def _():
                for k in range(nkc):
                    xb[k, q, :] = stage[q, k * tk:(k + 1) * tk].astype(BF16)

            @pl.when(b >= nb)
            def _():
                for k in range(nkc):
                    xb[k, q, :] = jnp.zeros((bm, tk), BF16)

        @pl.when(i + 1 < pl.num_programs(0))
        def _():
            for_rows(i + 1, lambda r, t: _row_copy(x_hbm, t, stage, r, sem).start())

    def hidden(m):
        x = xb[c, 0:m, :]
        a = jnp.dot(x, w1_ref[...].astype(BF16), preferred_element_type=F32)
        g = jnp.dot(x, w3_ref[...].astype(BF16), preferred_element_type=F32)

        def finish(at, gt):
            hb[0:m, :] = ((at * jax.nn.sigmoid(at)) * gt).astype(BF16)

        if nkc == 1:
            finish(a, g)
            return

        @pl.when(c == 0)
        def _():
            acc_a[0:m, :] = a
            acc_g[0:m, :] = g

        @pl.when(jnp.logical_and(c > 0, c < nkc - 1))
        def _():
            acc_a[0:m, :] += a
            acc_g[0:m, :] += g

        @pl.when(c == nkc - 1)
        def _():
            finish(acc_a[0:m, :] + a, acc_g[0:m, :] + g)

    def project(m):
        h = hb[0:m, :]
        lo = jnp.dot(h, w2_ref[...].astype(BF16), preferred_element_type=F32)
        hi = jnp.dot(h, w2b_ref[...].astype(BF16), preferred_element_type=F32)
        o_ref[0:m, :] = _pack_bf16_pair(lo, hi)
        if m < rmax:
            o_ref[m:rmax, :] = jnp.zeros((rmax - m, o_ref.shape[1]), jnp.uint32)

    small = (rb - 1) * bm
    for cond, m in ((jnp.logical_and(nb > 0, nb < rb), small), (nb == rb, rmax)):
        @pl.when(jnp.logical_and(cond, c < nkc))
        def _():
            hidden(m)

        @pl.when(jnp.logical_and(cond, c >= nkc))
        def _():
            project(m)

    @pl.when(jnp.logical_and(nb == 0, c >= nkc))
    def _():
        o_ref[...] = jnp.zeros_like(o_ref)


def _experts(x, sb_e, sb_nb, n_used, slot_tok, w1, w3, w2, bm, rb):
    n, d = x.shape
    e, _, hd = w1.shape
    n_sb = sb_e.shape[0]
    rmax = rb * bm
    dh = d // 2
    tk, oc = min(MOE_KC, d), min(MOE_OC, dh)
    nkc, noc = d // tk, dh // oc

    def w13_map(i, c, se, nb, nu, tok):
        return se[i], jnp.where(i < nu[0], jnp.minimum(c, nkc - 1), nkc - 1), 0

    def w2_chunk(i, c, nu):
        return jnp.where(i < nu[0], jnp.clip(c - nkc, 0, noc - 1), noc - 1)

    def w2_map(i, c, se, nb, nu, tok):
        return se[i], 0, w2_chunk(i, c, nu)

    def w2b_map(i, c, se, nb, nu, tok):
        return se[i], 0, noc + w2_chunk(i, c, nu)

    def out_map(i, c, se, nb, nu, tok):
        writing = c >= nkc
        row = jnp.where(writing, i, jnp.maximum(i - 1, 0))
        col = jnp.where(writing, c - nkc, jnp.where(i == 0, 0, noc - 1))
        return row, col

    grid_spec = pltpu.PrefetchScalarGridSpec(
        num_scalar_prefetch=4,
        grid=(n_sb, nkc + noc),
        in_specs=[pl.BlockSpec(memory_space=pl.ANY),
                  pl.BlockSpec((None, tk, hd), w13_map),
                  pl.BlockSpec((None, tk, hd), w13_map),
                  pl.BlockSpec((None, hd, oc), w2_map),
                  pl.BlockSpec((None, hd, oc), w2b_map)],
        out_specs=pl.BlockSpec((rmax, oc), out_map),
        scratch_shapes=[pltpu.VMEM((rmax, d), F32), pltpu.VMEM((nkc, rmax, tk), BF16),
                        pltpu.VMEM((rmax, hd), F32), pltpu.VMEM((rmax, hd), F32),
                        pltpu.VMEM((rmax, hd), BF16), pltpu.SemaphoreType.DMA(())],
    )
    return pl.pallas_call(
        functools.partial(_expert_body, bm=bm, rb=rb, nkc=nkc),
        grid_spec=grid_spec,
        out_shape=jax.ShapeDtypeStruct((n_sb * rmax, dh), jnp.uint32),
        compiler_params=_cparams("arbitrary", "arbitrary"),
        name="moe_experts",
    )(sb_e, sb_nb, n_used, slot_tok, x, w1, w3, w2, w2)


def _combine_body(slot_ref, y_hbm, w_ref, x_ref, g_ref, b_ref, o_ref, buf, sem, *, tb):
    i = pl.program_id(0)

    def gather(tile, half):
        def issue(r, c):
            for kk in range(MOE_TOPK):
                _row_copy(y_hbm, slot_ref[(tile * tb + r) * MOE_TOPK + kk], buf.at[half, kk], r,
                          sem.at[half]).start(priority=kk % 2)
            return c
        lax.fori_loop(0, tb, issue, 0, unroll=8)

    @pl.when(i == 0)
    def _():
        gather(0, 0)

    @pl.when(i + 1 < pl.num_programs(0))
    def _():
        gather(i + 1, (i + 1) % 2)

    half = i % 2

    def wait(r, c):
        for kk in range(MOE_TOPK):
            _row_copy(y_hbm, 0, buf.at[half, kk], r, sem.at[half]).wait()
        return c
    lax.fori_loop(0, tb, wait, 0, unroll=8)
    w = w_ref[...]
    lo0, hi0 = _unpack_bf16_pair(buf[half, 0])
    lo1, hi1 = _unpack_bf16_pair(buf[half, 1])
    y = jnp.concatenate([w[:, 0:1] * lo0 + w[:, 1:2] * lo1, w[:, 0:1] * hi0 + w[:, 1:2] * hi1], axis=1)
    o_ref[...] = _layer_norm(DEEPNORM_ALPHA * x_ref[...] + y, g_ref[...], b_ref[...])


def _combine_ln(slot_of, yb, weights, x, g, b, tb):
    n, d = x.shape
    grid_spec = pltpu.PrefetchScalarGridSpec(
        num_scalar_prefetch=1,
        grid=(n // tb,),
        in_specs=[pl.BlockSpec(memory_space=pl.ANY),
                  pl.BlockSpec((tb, LANES), lambda i, s: (i, 0)),
                  pl.BlockSpec((tb, d), lambda i, s: (i, 0)),
                  pl.BlockSpec((1, d), lambda i, s: (0, 0)),
                  pl.BlockSpec((1, d), lambda i, s: (0, 0))],
        out_specs=pl.BlockSpec((tb, d), lambda i, s: (i, 0)),
        scratch_shapes=[pltpu.VMEM((2, MOE_TOPK, tb, d // 2), jnp.uint32),
                        pltpu.SemaphoreType.DMA((2,))],
    )
    return pl.pallas_call(
        functools.partial(_combine_body, tb=tb),
        grid_spec=grid_spec,
        out_shape=jax.ShapeDtypeStruct((n, d), F32),
        compiler_params=_cparams("arbitrary"),
        name="moe_combine_ln",
    )(slot_of, yb, weights, x, g.reshape(1, d), b.reshape(1, d))


def _ln_moe_ln(x, mix, ln1_g, ln1_b, router_w1, router_b1, router_w2, router_b2, exp_w1, exp_w3, exp_w2,
               ln_g, ln_b):
    n, d = x.shape
    e = exp_w1.shape[0]
    bm = MOE_BM
    h1, weights, ids, cnt = _ln_router(x, mix, ln1_g, ln1_b, router_w1, router_b1, router_w2, router_b2)
    m = n * MOE_TOPK
    rb = MOE_RB
    rmax = rb * bm
    eid_f = ids[:, :MOE_TOPK].reshape(m)
    rank = ids[:, MOE_TOPK:2 * MOE_TOPK].reshape(m)
    counts = cnt[0, MOE_GROUPS:MOE_GROUPS + e].astype(jnp.int32)
    nblk_e = (counts + bm - 1) // bm
    ngrp_e = (nblk_e + rb - 1) // rb
    gend = jnp.cumsum(ngrp_e)
    gstart = gend - ngrp_e
    slot_of = ((gstart[eid_f] + rank // rmax) * rmax + rank % rmax).astype(jnp.int32)
    n_grp = -(-(-(-m // bm) + e * rb) // rb)
    tok_f = jnp.arange(m, dtype=jnp.int32) // MOE_TOPK
    slot_tok = jnp.zeros((n_grp * rmax,), jnp.int32).at[slot_of].set(tok_f)
    n_used = gend[-1:].astype(jnp.int32)
    gidx = jnp.arange(n_grp, dtype=jnp.int32)
    gcl = jnp.minimum(gidx, n_used[0] - 1)
    grp_e = jnp.minimum(jnp.sum(gend[None, :] <= gcl[:, None], axis=1), e - 1).astype(jnp.int32)
    local = gcl - gstart[grp_e]
    grp_nb = jnp.where(gidx < n_used[0], jnp.minimum(rb, nblk_e[grp_e] - local * rb), 0).astype(jnp.int32)
    yb = _experts(h1, grp_e, grp_nb, n_used, slot_tok, exp_w1, exp_w3, exp_w2, bm, rb)
    return _combine_ln(slot_of, yb, weights, h1, ln_g, ln_b, min(MOE_TB, n))


def kernel(x, w_in, hy_conv_w, hy_conv_b, hy_f_w1, hy_f_b1, hy_f_fr1, hy_f_w2, hy_f_b2, hy_f_fr2,
           hy_f_w3, hy_f_b3, hy_f_fr3, hy_f_wout, hy_bias, ml_gate_bias, ml_norm_g, p_hy, p_ml, w_out,
           ln1_g, ln1_b, router_w1, router_b1, router_w2, router_b2, exp_w1, exp_w3, exp_w2,
           ln2_g, ln2_b):
    B, L, D = x.shape
    N = B * L
    C = D // 2
    H = ML_HEADS
    dv = C // H
    dk = dv // 2
    col_q = (HY_ORDER + 1) * C
    col_if = col_q + 2 * H * dk + 2 * C
    col_gate = col_if + 4 * H

    xf = x.reshape(N, D)
    xb = xf.astype(BF16)
    w_hy = w_in[:, :col_q].astype(BF16)
    w_ml = w_in[:, col_q:col_if].astype(BF16)
    w_if = jnp.zeros((D, LANES), BF16).at[:, :4 * H].set(w_in[:, col_if:col_gate].astype(BF16))
    w_gt = w_in[:, col_gate:].astype(BF16)

    z_hy = _matmul(xb, w_hy, F32, 1024, 512).reshape(B, L, col_q)
    z_ml = _matmul(xb, w_ml, BF16, 1024, 512)
    z_if = _matmul(xb, w_if, F32, 1024, LANES)
    z_gt = _matmul(xb, w_gt, BF16, 1024, 512)

    tables, filter_tables = _dft_tables(L)
    kern = _hyena_filters(L, C, hy_f_w1, hy_f_b1, hy_f_fr1, hy_f_w2, hy_f_b2, hy_f_fr2,
                          hy_f_w3, hy_f_b3, hy_f_fr3, hy_f_wout, hy_bias)
    n2 = 2 * L // FFT_N1
    kfreq = _filter_fft(_row_permute(kern, FFT_N1, n2, F32), *filter_tables)
    u = _short_conv(z_hy, hy_conv_w, hy_conv_b, n2)
    v1 = _long_conv(u, 2, u, 0, kfreq, 0, tables, C)
    y_hy = _long_conv(v1, 0, u, 1, kfreq, 1, tables, C)
    y_hy = _row_permute(y_hy, n2, L // n2, BF16).reshape(N, C)

    g = z_if[:, :4 * H].reshape(B, L, 4, H)
    gates_c = g.transpose(0, 3, 1, 2)
    gates_r = g.transpose(0, 3, 2, 1)
    bias_c = ml_gate_bias.T.reshape(H, 1, 4)
    bias_r = ml_gate_bias.T.reshape(H, 4, 1)
    h_fwd = _mlstm(z_ml, gates_c, gates_r, bias_c, bias_r, B, L, dk, dv, rev=False)
    y_ml = _mlstm(z_ml, gates_c, gates_r, bias_c, bias_r, B, L, dk, dv, rev=True,
                  hprev=h_fwd, norm_g=ml_norm_g.reshape(1, C))

    merged = _merge(y_hy, y_ml, p_hy.astype(BF16), p_ml.astype(BF16), z_gt)
    mix = _matmul(merged, w_out.astype(BF16), BF16, 1024, 512)
    out = _ln_moe_ln(xf, mix, ln1_g, ln1_b, router_w1, router_b1, router_w2, router_b2,
                     exp_w1, exp_w3, exp_w2, ln2_g, ln2_b)
    return out.reshape(B, L, D)
```
